```python
import math
import jax, jax.numpy as jnp
from jax import lax
import numpy as np

D_MODEL = 1024
BATCH = 8
SEQ = 4096
DEPTH = 4
DEC_BATCH = 2
DEC_SEQ = 8192
PAST_LEN = 128

N_MIXERS = 2
N_A_LAYERS = (DEPTH + 1) // 2
N_B_LAYERS = DEPTH // 2

A_HEADS = 8
A_Q_LORA = 384
A_KV_LORA = 256
A_NOPE = 128
A_ROPE = 64
A_V = 128
A_WIDTH = A_HEADS * A_V
A_IN = A_Q_LORA + A_KV_LORA + A_ROPE + A_WIDTH
ROPE_THETA = 10000.0
Q_BLOCK = 128

B_WIDTH = 2 * D_MODEL
B_GROUPS = 8
B_GROUP_DIM = B_WIDTH // B_GROUPS
B_CHUNK = 128
B_IN = 3 * B_WIDTH

RMS_EPS = 1e-6
LN_EPS = 1e-5

kernel_name = "hybrid_mla_gmlp_encoder"


def rmsnorm(x, g):
    x32 = x.astype(jnp.float32)
    y = x32 * lax.rsqrt(jnp.mean(x32 * x32, axis=-1, keepdims=True) + RMS_EPS)
    return y.astype(x.dtype) * g


def layernorm(x, g, b):
    x32 = x.astype(jnp.float32)
    mu = jnp.mean(x32, axis=-1, keepdims=True)
    xc = x32 - mu
    y = xc * lax.rsqrt(jnp.mean(xc * xc, axis=-1, keepdims=True) + LN_EPS)
    return y.astype(x.dtype) * g + b


def rope_tables(seq, dtype):
    inv = ROPE_THETA ** (-jnp.arange(0, A_ROPE, 2, dtype=jnp.float32) / A_ROPE)
    ang = jnp.arange(seq, dtype=jnp.float32)[:, None] * inv[None, :]
    ang = jnp.concatenate([ang, ang], axis=-1)
    return jnp.cos(ang).astype(dtype), jnp.sin(ang).astype(dtype)


def apply_rope(x, cos, sin):
    half = A_ROPE // 2
    x1, x2 = x[..., :half], x[..., half:]
    rot = jnp.concatenate([-x2, x1], axis=-1)
    return x * cos + rot * sin


def mla_attention(q_nope, q_rope, k_nope, k_rope, v):
    bsz, seq = q_nope.shape[0], q_nope.shape[1]
    nb = seq // Q_BLOCK
    scale = (A_NOPE + A_ROPE) ** -0.5

    def blocks(t):
        return jnp.moveaxis(t.reshape(bsz, nb, Q_BLOCK, *t.shape[2:]), 1, 0)

    def one_block(qb):
        qn, qr = qb
        s = (jnp.einsum('bqhd,bkhd->bhqk', qn, k_nope)
             + jnp.einsum('bqhr,bkr->bhqk', qr, k_rope))
        p = jax.nn.softmax(s.astype(jnp.float32) * scale, axis=-1).astype(v.dtype)
        return jnp.einsum('bhqk,bkhd->bqhd', p, v)

    o = lax.map(one_block, (blocks(q_nope), blocks(q_rope)))
    return jnp.moveaxis(o, 0, 1).reshape(bsz, seq, A_WIDTH)


def mla_layer(x, g, w_in, q_norm, kv_norm, w_q_up, w_kv_up, w_out):
    bsz, seq, _ = x.shape
    h = rmsnorm(x, g)
    proj = h @ w_in
    q_lat, kv_lat, k_rope, gate = jnp.split(
        proj, [A_Q_LORA, A_Q_LORA + A_KV_LORA, A_Q_LORA + A_KV_LORA + A_ROPE], axis=-1)
    q = (rmsnorm(q_lat, q_norm) @ w_q_up).reshape(bsz, seq, A_HEADS, A_NOPE + A_ROPE)
    kv = (rmsnorm(kv_lat, kv_norm) @ w_kv_up).reshape(bsz, seq, A_HEADS, A_NOPE + A_V)
    q_nope, q_rope = q[..., :A_NOPE], q[..., A_NOPE:]
    k_nope, v = kv[..., :A_NOPE], kv[..., A_NOPE:]
    cos, sin = rope_tables(seq, x.dtype)
    q_rope = apply_rope(q_rope, cos[:, None, :], sin[:, None, :])
    k_rope = apply_rope(k_rope, cos, sin)
    o = mla_attention(q_nope, q_rope, k_nope, k_rope, v)
    return x + (o * jax.nn.silu(gate)) @ w_out


def gmlp_layer(x, g, w_in, ln_g, ln_b, w_s, b_s, w_out):
    bsz, seq, _ = x.shape
    nc = seq // B_CHUNK
    h = rmsnorm(x, g)
    u, v, gate = jnp.split(h @ w_in, 3, axis=-1)
    u = jax.nn.gelu(u, approximate=False)
    v = layernorm(jax.nn.gelu(v, approximate=False), ln_g, ln_b)
    vc = v.reshape(bsz, nc, B_CHUNK, B_GROUPS, B_GROUP_DIM)
    sv = jnp.einsum('gpq,bnqgd->bnpgd', w_s, vc) + jnp.transpose(b_s)[:, :, None]
    s = u * sv.reshape(bsz, seq, B_WIDTH)
    return x + (s * jax.nn.silu(gate)) @ w_out


def trunk(x, norm_g, final_g,
          a_w_in, a_q_norm, a_kv_norm, a_w_q_up, a_w_kv_up, a_w_out,
          b_w_in, b_ln_g, b_ln_b, b_w_s, b_b_s, b_w_out):
    for i in range(DEPTH):
        j = i // N_MIXERS
        if i % N_MIXERS == 0:
            x = mla_layer(x, norm_g[i], a_w_in[j], a_q_norm[j], a_kv_norm[j],
                          a_w_q_up[j], a_w_kv_up[j], a_w_out[j])
        else:
            x = gmlp_layer(x, norm_g[i], b_w_in[j], b_ln_g[j], b_ln_b[j],
                           b_w_s[j], b_b_s[j], b_w_out[j])
    return rmsnorm(x, final_g)


def setup_inputs(seed: int = 0) -> dict:
    key = jax.random.key(seed)
    ks = jax.random.split(key, 20)
    f32 = jnp.float32

    def nrm(k, shape, scale):
        return jax.random.normal(k, shape, f32) * scale

    return {
        "x_prompt": nrm(ks[0], (BATCH, SEQ, D_MODEL), 1.0),
        "x_sample": nrm(ks[1], (DEC_BATCH, DEC_SEQ, D_MODEL), 1.0),
        "norm_g": 1.0 + nrm(ks[2], (DEPTH, D_MODEL), 0.02),
        "final_g": 1.0 + nrm(ks[3], (D_MODEL,), 0.02),
        "a_w_in": nrm(ks[4], (N_A_LAYERS, D_MODEL, A_IN), D_MODEL ** -0.5),
        "a_q_norm": 1.0 + nrm(ks[5], (N_A_LAYERS, A_Q_LORA), 0.02),
        "a_kv_norm": 1.0 + nrm(ks[6], (N_A_LAYERS, A_KV_LORA), 0.02),
        "a_w_q_up": nrm(ks[7], (N_A_LAYERS, A_Q_LORA, A_HEADS * (A_NOPE + A_ROPE)), A_Q_LORA ** -0.5),
        "a_w_kv_up": nrm(ks[8], (N_A_LAYERS, A_KV_LORA, A_HEADS * (A_NOPE + A_V)), A_KV_LORA ** -0.5),
        "a_w_out": nrm(ks[9], (N_A_LAYERS, A_WIDTH, D_MODEL), A_WIDTH ** -0.5),
        "b_w_in": nrm(ks[10], (N_B_LAYERS, D_MODEL, B_IN), D_MODEL ** -0.5),
        "b_ln_g": 1.0 + nrm(ks[11], (N_B_LAYERS, B_WIDTH), 0.02),
        "b_ln_b": nrm(ks[12], (N_B_LAYERS, B_WIDTH), 0.02),
        "b_w_s": nrm(ks[13], (N_B_LAYERS, B_GROUPS, B_CHUNK, B_CHUNK), B_CHUNK ** -0.5),
        "b_b_s": 1.0 + nrm(ks[14], (N_B_LAYERS, B_GROUPS, B_CHUNK), 0.02),
        "b_w_out": nrm(ks[15], (N_B_LAYERS, B_WIDTH, D_MODEL), B_WIDTH ** -0.5),
    }


def reference(x_prompt, x_sample, norm_g, final_g,
              a_w_in, a_q_norm, a_kv_norm, a_w_q_up, a_w_kv_up, a_w_out,
              b_w_in, b_ln_g, b_ln_b, b_w_s, b_b_s, b_w_out):
    y_prompt = trunk(x_prompt, norm_g, final_g,
                     a_w_in, a_q_norm, a_kv_norm, a_w_q_up, a_w_kv_up, a_w_out,
                     b_w_in, b_ln_g, b_ln_b, b_w_s, b_b_s, b_w_out)
    y_sample = trunk(x_sample, norm_g, final_g,
                     a_w_in, a_q_norm, a_kv_norm, a_w_q_up, a_w_kv_up, a_w_out,
                     b_w_in, b_ln_g, b_ln_b, b_w_s, b_b_s, b_w_out)
    return (y_prompt, y_sample)
```

```python
import functools
import math

import jax
import jax.numpy as jnp
from jax import lax
from jax.experimental import pallas as pl
from jax.experimental.pallas import tpu as pltpu

D_MODEL = 1024
DEPTH = 4

A_HEADS = 8
A_Q_LORA = 384
A_KV_LORA = 256
A_NOPE = 128
A_ROPE = 64
A_V = 128
A_WIDTH = A_HEADS * A_V
ROPE_THETA = 10000.0
A_QK_PAD = 256

B_WIDTH = 2 * D_MODEL
B_GROUPS = 8
B_GROUP_DIM = B_WIDTH // B_GROUPS
B_CHUNK = 128

RMS_EPS = 1e-6
LN_EPS = 1e-5

LANES = 128
PROJ_TM = 512
ATTN_TQ = 512
ATTN_TK = 512
GMLP_TM = 256
VMEM_LIMIT = 56 * 1024 * 1024

_C_Q = 0
_C_KV = _C_Q + A_Q_LORA
_C_GATE = _C_KV + A_KV_LORA
_C_KR = _C_GATE + A_WIDTH
_C_KRR = _C_KR + LANES
A_IN_PAD = _C_KRR + LANES

_NT = (((1,), (1,)), ((), ()))


def _const_spec(shape):
    return pl.BlockSpec(shape, lambda *_: (0,) * len(shape), pipeline_mode=pl.Buffered(1))


def _rms(x, g):
    return (x * lax.rsqrt(jnp.mean(x * x, axis=-1, keepdims=True) + RMS_EPS)) * g


def _mla_proj_kernel(x_ref, g_ref, w_in_ref, qn_ref, kvn_ref, wq_t_ref, wk_ref, wv_t_ref,
                     cos_t_ref, sin_t_ref, cos_k_ref, sin_k_ref,
                     q_t_ref, k_ref, v_t_ref, gate_ref):
    hb = _rms(x_ref[...], g_ref[...]).astype(jnp.bfloat16)
    proj = jnp.dot(hb, w_in_ref[...], preferred_element_type=jnp.float32)
    gate_ref[...] = proj[:, _C_GATE:_C_KR]

    q_lat = _rms(proj[:, _C_Q:_C_KV], qn_ref[...]).astype(jnp.bfloat16)
    kv_lat = _rms(proj[:, _C_KV:_C_GATE], kvn_ref[...]).astype(jnp.bfloat16)

    kr = (proj[:, _C_KR:_C_KRR] * cos_k_ref[...]
          + proj[:, _C_KRR:A_IN_PAD] * sin_k_ref[...]).astype(jnp.bfloat16)

    q_t = lax.dot_general(wq_t_ref[...], q_lat, _NT, preferred_element_type=jnp.float32)
    k_nope = jnp.dot(kv_lat, wk_ref[...], preferred_element_type=jnp.float32)
    v_t = lax.dot_general(wv_t_ref[...], kv_lat, _NT, preferred_element_type=jnp.float32)

    cos_t = cos_t_ref[...]
    sin_t = sin_t_ref[...]
    half = A_ROPE // 2
    n_chunks = v_t_ref.shape[1]
    tk = v_t_ref.shape[3]
    rope0 = A_HEADS * A_NOPE
    for h in range(A_HEADS):
        k_ref[h, :, 0:A_NOPE] = k_nope[:, h * A_NOPE:(h + 1) * A_NOPE].astype(jnp.bfloat16)
        k_ref[h, :, A_NOPE:A_QK_PAD] = kr
        q_t_ref[h, 0:A_NOPE, :] = q_t[h * A_NOPE:(h + 1) * A_NOPE, :].astype(jnp.bfloat16)
        qr = q_t[rope0 + h * A_ROPE: rope0 + (h + 1) * A_ROPE, :]
        rot = jnp.concatenate([qr[half:, :], qr[:half, :]], axis=0)
        q_t_ref[h, A_NOPE:A_NOPE + A_ROPE, :] = (qr * cos_t + rot * sin_t).astype(jnp.bfloat16)
        q_t_ref[h, A_NOPE + A_ROPE:A_QK_PAD, :] = jnp.zeros(
            (A_QK_PAD - A_NOPE - A_ROPE, q_t.shape[1]), jnp.bfloat16)
        for c in range(n_chunks):
            v_t_ref[h, c] = v_t[h * A_V:(h + 1) * A_V, c * tk:(c + 1) * tk].astype(jnp.bfloat16)


def _mla_proj(x, g, w_in, qn, kvn, wq_t, wk, wv_t, cos_t, sin_t, cos_k, sin_k):
    bsz, seq, _ = x.shape
    tm = PROJ_TM
    n_chunks = tm // ATTN_TK
    grid = (bsz, seq // tm)
    return pl.pallas_call(
        _mla_proj_kernel,
        grid=grid,
        in_specs=[
            pl.BlockSpec((None, tm, D_MODEL), lambda b, i: (b, i, 0)),
            _const_spec(g.shape), _const_spec(w_in.shape), _const_spec(qn.shape), _const_spec(kvn.shape),
            _const_spec(wq_t.shape), _const_spec(wk.shape), _const_spec(wv_t.shape),
            pl.BlockSpec((A_ROPE, tm), lambda b, i: (0, i)),
            pl.BlockSpec((A_ROPE, tm), lambda b, i: (0, i)),
            pl.BlockSpec((tm, LANES), lambda b, i: (i, 0)),
            pl.BlockSpec((tm, LANES), lambda b, i: (i, 0)),
        ],
        out_specs=[
            pl.BlockSpec((None, A_HEADS, A_QK_PAD, tm), lambda b, i: (b, 0, 0, i)),
            pl.BlockSpec((None, A_HEADS, tm, A_QK_PAD), lambda b, i: (b, 0, i, 0)),
            pl.BlockSpec((None, A_HEADS, n_chunks, A_V, ATTN_TK), lambda b, i: (b, 0, i, 0, 0)),
            pl.BlockSpec((None, tm, A_WIDTH), lambda b, i: (b, i, 0)),
        ],
        out_shape=[
            jax.ShapeDtypeStruct((bsz, A_HEADS, A_QK_PAD, seq), jnp.bfloat16),
            jax.ShapeDtypeStruct((bsz, A_HEADS, seq, A_QK_PAD), jnp.bfloat16),
            jax.ShapeDtypeStruct((bsz, A_HEADS, seq // ATTN_TK, A_V, ATTN_TK), jnp.bfloat16),
            jax.ShapeDtypeStruct((bsz, seq, A_WIDTH), jnp.float32),
        ],
        compiler_params=pltpu.CompilerParams(
            dimension_semantics=("parallel", "parallel"), vmem_limit_bytes=VMEM_LIMIT),
        name="mla_proj",
    )(x, g, w_in, qn, kvn, wq_t, wk, wv_t, cos_t, sin_t, cos_k, sin_k)


def _mla_attn_kernel(q_t_ref, k_ref, v_t_ref, gate_ref, o_ref, m_ref, l_ref, acc_ref, *, exp2_scale):
    n_chunks = v_t_ref.shape[0]
    tk = v_t_ref.shape[2]
    q_t = q_t_ref[...]
    m_ref[...] = jnp.full(m_ref.shape, -jnp.inf, jnp.float32)
    l_ref[...] = jnp.zeros(l_ref.shape, jnp.float32)
    acc_ref[...] = jnp.zeros(acc_ref.shape, jnp.float32)

    def chunk(j, carry):
        k_j = k_ref[pl.ds(pl.multiple_of(j * tk, tk), tk), :]
        s = jnp.dot(k_j, q_t, preferred_element_type=jnp.float32)
        m_old = m_ref[...]
        m_new = jnp.maximum(m_old, jnp.max(s, axis=0, keepdims=True))
        alpha = jnp.exp2((m_old - m_new) * exp2_scale)
        p = jnp.exp2((s - m_new) * exp2_scale)
        l_ref[...] = alpha * l_ref[...] + jnp.sum(p, axis=0, keepdims=True)
        acc_ref[...] = alpha * acc_ref[...] + jnp.dot(
            v_t_ref[j], p.astype(jnp.bfloat16), preferred_element_type=jnp.float32)
        m_ref[...] = m_new
        return carry

    lax.fori_loop(0, n_chunks, chunk, 0)
    o = (acc_ref[...] / l_ref[...]).T
    gate = gate_ref[...]
    o_ref[...] = (o * (gate * jax.nn.sigmoid(gate))).astype(o_ref.dtype)


def _mla_attn(q_t, k, v_t, gate):
    bsz, _, _, seq = q_t.shape
    tq = ATTN_TQ
    n_chunks = seq // ATTN_TK
    exp2_scale = (A_NOPE + A_ROPE) ** -0.5 * math.log2(math.e)
    return pl.pallas_call(
        functools.partial(_mla_attn_kernel, exp2_scale=exp2_scale),
        grid=(bsz, A_HEADS, seq // tq),
        in_specs=[
            pl.BlockSpec((None, None, A_QK_PAD, tq), lambda b, h, i: (b, h, 0, i)),
            pl.BlockSpec((None, None, seq, A_QK_PAD), lambda b, h, i: (b, h, 0, 0)),
            pl.BlockSpec((None, None, n_chunks, A_V, ATTN_TK), lambda b, h, i: (b, h, 0, 0, 0)),
            pl.BlockSpec((None, tq, A_V), lambda b, h, i: (b, i, h)),
        ],
        out_specs=pl.BlockSpec((None, tq, A_V), lambda b, h, i: (b, i, h)),
        out_shape=jax.ShapeDtypeStruct((bsz, seq, A_WIDTH), jnp.bfloat16),
        scratch_shapes=[
            pltpu.VMEM((1, tq), jnp.float32),
            pltpu.VMEM((1, tq), jnp.float32),
            pltpu.VMEM((A_V, tq), jnp.float32),
        ],
        compiler_params=pltpu.CompilerParams(
            dimension_semantics=("parallel", "parallel", "arbitrary"), vmem_limit_bytes=VMEM_LIMIT),
        name="mla_attn",
    )(q_t, k, v_t, gate)


def _gelu(x):
    return 0.5 * x * (1.0 + lax.erf(x * (1.0 / math.sqrt(2.0))))


def _gmlp_kernel(x_ref, og_ref, w_oa_ref, g_ref, w_in_ref, ln_g_ref, ln_b_ref, w_s_ref, b_s_ref,
                 w_out_ref, fg_ref, y_ref, s_ref, *, final_norm):
    tm = x_ref.shape[0]
    x1 = x_ref[...] + jnp.dot(og_ref[...], w_oa_ref[...], preferred_element_type=jnp.float32)
    hb = _rms(x1, g_ref[...]).astype(jnp.bfloat16)
    uvg = jnp.dot(hb, w_in_ref[...], preferred_element_type=jnp.float32)
    u = _gelu(uvg[:, :B_WIDTH])
    v = _gelu(uvg[:, B_WIDTH:2 * B_WIDTH])
    gate = uvg[:, 2 * B_WIDTH:]
    mu = jnp.mean(v, axis=-1, keepdims=True)
    vc = v - mu
    vn = (vc * lax.rsqrt(jnp.mean(vc * vc, axis=-1, keepdims=True) + LN_EPS)) * ln_g_ref[...] + ln_b_ref[...]
    vn = vn.astype(jnp.bfloat16)
    ug = u * (gate * jax.nn.sigmoid(gate))
    for c in range(tm // B_CHUNK):
        rows = slice(c * B_CHUNK, (c + 1) * B_CHUNK)
        for grp in range(B_GROUPS):
            cols = slice(grp * B_GROUP_DIM, (grp + 1) * B_GROUP_DIM)
            bias = b_s_ref[grp]
            sv = jnp.dot(w_s_ref[grp], vn[rows, cols], preferred_element_type=jnp.float32)
            sv = sv + jnp.concatenate([bias] * (B_GROUP_DIM // LANES), axis=1)
            s_ref[rows, cols] = (ug[rows, cols] * sv).astype(jnp.bfloat16)
    y = x1 + jnp.dot(s_ref[...], w_out_ref[...], preferred_element_type=jnp.float32)
    if final_norm:
        y = _rms(y, fg_ref[...])
    y_ref[...] = y


def _gmlp(x, og, w_oa, g, w_in, ln_g, ln_b, w_s, b_s, w_out, fg, *, final_norm):
    bsz, seq, _ = x.shape
    tm = GMLP_TM
    return pl.pallas_call(
        functools.partial(_gmlp_kernel, final_norm=final_norm),
        grid=(bsz, seq // tm),
        in_specs=[
            pl.BlockSpec((None, tm, D_MODEL), lambda b, i: (b, i, 0)),
            pl.BlockSpec((None, tm, A_WIDTH), lambda b, i: (b, i, 0)),
            _const_spec(w_oa.shape), _const_spec(g.shape), _const_spec(w_in.shape),
            _const_spec(ln_g.shape), _const_spec(ln_b.shape), _const_spec(w_s.shape),
            _const_spec(b_s.shape), _const_spec(w_out.shape), _const_spec(fg.shape),
        ],
        out_specs=pl.BlockSpec((None, tm, D_MODEL), lambda b, i: (b, i, 0)),
        out_shape=jax.ShapeDtypeStruct((bsz, seq, D_MODEL), jnp.float32),
        scratch_shapes=[pltpu.VMEM((tm, B_WIDTH), jnp.bfloat16)],
        compiler_params=pltpu.CompilerParams(
            dimension_semantics=("parallel", "parallel"), vmem_limit_bytes=VMEM_LIMIT),
        name="gmlp",
    )(x, og, w_oa, g, w_in, ln_g, ln_b, w_s, b_s, w_out, fg)


def _prep_mla(w_in, q_norm, kv_norm, w_q_up, w_kv_up, w_out):
    bf = jnp.bfloat16
    half = A_ROPE // 2
    c_kr = A_Q_LORA + A_KV_LORA
    w_kr = w_in[:, c_kr:c_kr + A_ROPE]
    w_kr_rot = jnp.concatenate([-w_kr[:, half:], w_kr[:, :half]], axis=1)
    zpad = jnp.zeros((D_MODEL, LANES - A_ROPE), w_in.dtype)
    w_in_p = jnp.concatenate(
        [w_in[:, :c_kr], w_in[:, c_kr + A_ROPE:], w_kr, zpad, w_kr_rot, zpad], axis=1).astype(bf)
    wq = w_q_up.reshape(A_Q_LORA, A_HEADS, A_NOPE + A_ROPE)
    wq_t = jnp.concatenate(
        [wq[:, :, :A_NOPE].reshape(A_Q_LORA, -1), wq[:, :, A_NOPE:].reshape(A_Q_LORA, -1)], axis=1).T.astype(bf)
    wkv = w_kv_up.reshape(A_KV_LORA, A_HEADS, A_NOPE + A_V)
    wk = wkv[:, :, :A_NOPE].reshape(A_KV_LORA, -1).astype(bf)
    wv_t = wkv[:, :, A_NOPE:].reshape(A_KV_LORA, -1).T.astype(bf)
    return (w_in_p, q_norm.reshape(1, -1), kv_norm.reshape(1, -1), wq_t, wk, wv_t, w_out.astype(bf))


def _prep_gmlp(w_in, ln_g, ln_b, w_s, b_s, w_out):
    bf = jnp.bfloat16
    b_rep = jnp.broadcast_to(b_s[:, :, None], (B_GROUPS, B_CHUNK, LANES))
    return (w_in.astype(bf), ln_g.reshape(1, -1), ln_b.reshape(1, -1), w_s.astype(bf), b_rep, w_out.astype(bf))


def _rope_tables(seq):
    inv = ROPE_THETA ** (-jnp.arange(0, A_ROPE, 2, dtype=jnp.float32) / A_ROPE)
    ang = jnp.arange(seq, dtype=jnp.float32)[:, None] * inv[None, :]
    ang = jnp.concatenate([ang, ang], axis=-1)
    cos, sin = jnp.cos(ang), jnp.sin(ang)
    half = A_ROPE // 2
    sign = jnp.concatenate([-jnp.ones((half,), jnp.float32), jnp.ones((half,), jnp.float32)])
    zpad = jnp.zeros((seq, LANES - A_ROPE), jnp.float32)
    cos_k = jnp.concatenate([cos, zpad], axis=1)
    sin_k = jnp.concatenate([sin, zpad], axis=1)
    return cos.T, (sin * sign).T, cos_k, sin_k


def _trunk(x, norm_g, final_g, mla_params, gmlp_params):
    seq = x.shape[1]
    tables = _rope_tables(seq)
    fg = final_g.reshape(1, -1)
    for j in range(DEPTH // 2):
        w_in_p, qn, kvn, wq_t, wk, wv_t, w_oa = mla_params[j]
        q_t, k, v_t, gate = _mla_proj(x, norm_g[2 * j].reshape(1, -1), w_in_p, qn, kvn, wq_t, wk, wv_t, *tables)
        og = _mla_attn(q_t, k, v_t, gate)
        x = _gmlp(x, og, w_oa, norm_g[2 * j + 1].reshape(1, -1), *gmlp_params[j], fg,
                  final_norm=(j == DEPTH // 2 - 1))
    return x


def kernel(x_prompt, x_sample, norm_g, final_g, a_w_in, a_q_norm, a_kv_norm, a_w_q_up, a_w_kv_up, a_w_out,
           b_w_in, b_ln_g, b_ln_b, b_w_s, b_b_s, b_w_out):
    mla_params = [_prep_mla(a_w_in[j], a_q_norm[j], a_kv_norm[j], a_w_q_up[j], a_w_kv_up[j], a_w_out[j])
                  for j in range(DEPTH // 2)]
    gmlp_params = [_prep_gmlp(b_w_in[j], b_ln_g[j], b_ln_b[j], b_w_s[j], b_b_s[j], b_w_out[j])
                   for j in range(DEPTH // 2)]
    y_prompt = _trunk(x_prompt, norm_g, final_g, mla_params, gmlp_params)
    y_sample = _trunk(x_sample, norm_g, final_g, mla_params, gmlp_params)
    return (y_prompt, y_sample)
```

```python
import functools
import math

import jax
import jax.numpy as jnp
from jax import lax
from jax.experimental import pallas as pl
from jax.experimental.pallas import tpu as pltpu

D_MODEL = 1024
DEPTH = 4

A_HEADS = 8
A_Q_LORA = 384
A_KV_LORA = 256
A_NOPE = 128
A_ROPE = 64
A_V = 128
A_WIDTH = A_HEADS * A_V
ROPE_THETA = 10000.0
A_QK_PAD = 256

B_WIDTH = 2 * D_MODEL
B_GROUPS = 8
B_GROUP_DIM = B_WIDTH // B_GROUPS
B_CHUNK = 128

RMS_EPS = 1e-6
LN_EPS = 1e-5

LANES = 128
SUBLANES = 8
ATTN_TQ = 512
ATTN_TK = 512
ATTN_DEPTH = 4
ATTN_B_LAG = 2
ATTN_C_LAG = 3
ATTN_RB = 64
PROJ_TM = ATTN_TQ
GMLP_TM = 256
VMEM_LIMIT = 56 * 1024 * 1024

_C_Q = 0
_C_KV = _C_Q + A_Q_LORA
_C_GATE = _C_KV + A_KV_LORA
_C_KR = _C_GATE + A_WIDTH
_C_KRR = _C_KR + LANES
A_IN_PAD = _C_KRR + LANES

_NT = (((1,), (1,)), ((), ()))


def _const_spec(shape):
    return pl.BlockSpec(shape, lambda *_: (0,) * len(shape), pipeline_mode=pl.Buffered(1))


def _rms(x, g):
    return (x * lax.rsqrt(jnp.mean(x * x, axis=-1, keepdims=True) + RMS_EPS)) * g


def _mla_proj_kernel(x_ref, g_ref, w_in_ref, qn_ref, kvn_ref, wq_t_ref, wk_ref, wv_t_ref,
                     cos_t_ref, sin_t_ref, cos_k_ref, sin_k_ref,
                     q_t_ref, k_ref, v_t_ref, sg_ref):
    hb = _rms(x_ref[...], g_ref[...]).astype(jnp.bfloat16)
    proj = jnp.dot(hb, w_in_ref[...], preferred_element_type=jnp.float32)
    gate = proj[:, _C_GATE:_C_KR]
    sg_ref[...] = gate * jax.nn.sigmoid(gate)

    q_lat = _rms(proj[:, _C_Q:_C_KV], qn_ref[...]).astype(jnp.bfloat16)
    kv_lat = _rms(proj[:, _C_KV:_C_GATE], kvn_ref[...]).astype(jnp.bfloat16)

    kr = (proj[:, _C_KR:_C_KRR] * cos_k_ref[...]
          + proj[:, _C_KRR:A_IN_PAD] * sin_k_ref[...]).astype(jnp.bfloat16)

    q_t = lax.dot_general(wq_t_ref[...], q_lat, _NT, preferred_element_type=jnp.float32)
    k_nope = jnp.dot(kv_lat, wk_ref[...], preferred_element_type=jnp.float32)
    v_t = lax.dot_general(wv_t_ref[...], kv_lat, _NT, preferred_element_type=jnp.float32)

    cos_t = cos_t_ref[...]
    sin_t = sin_t_ref[...]
    half = A_ROPE // 2
    rope0 = A_HEADS * A_NOPE
    for h in range(A_HEADS):
        k_ref[h, :, 0:A_NOPE] = k_nope[:, h * A_NOPE:(h + 1) * A_NOPE].astype(jnp.bfloat16)
        k_ref[h, :, A_NOPE:A_QK_PAD] = kr
        q_t_ref[h, 0:A_NOPE, :] = q_t[h * A_NOPE:(h + 1) * A_NOPE, :].astype(jnp.bfloat16)
        qr = q_t[rope0 + h * A_ROPE: rope0 + (h + 1) * A_ROPE, :]
        rot = jnp.concatenate([qr[half:, :], qr[:half, :]], axis=0)
        q_t_ref[h, A_NOPE:A_NOPE + A_ROPE, :] = (qr * cos_t + rot * sin_t).astype(jnp.bfloat16)
        q_t_ref[h, A_NOPE + A_ROPE:A_QK_PAD, :] = jnp.zeros(
            (A_QK_PAD - A_NOPE - A_ROPE, q_t.shape[1]), jnp.bfloat16)
        v_t_ref[h] = v_t[h * A_V:(h + 1) * A_V, :].astype(jnp.bfloat16)


def _mla_proj(x, g, w_in, qn, kvn, wq_t, wk, wv_t, cos_t, sin_t, cos_k, sin_k):
    bsz, seq, _ = x.shape
    tm = PROJ_TM
    n_tiles = seq // tm
    return pl.pallas_call(
        _mla_proj_kernel,
        grid=(bsz, n_tiles),
        in_specs=[
            pl.BlockSpec((None, tm, D_MODEL), lambda b, i: (b, i, 0)),
            _const_spec(g.shape), _const_spec(w_in.shape), _const_spec(qn.shape), _const_spec(kvn.shape),
            _const_spec(wq_t.shape), _const_spec(wk.shape), _const_spec(wv_t.shape),
            pl.BlockSpec((A_ROPE, tm), lambda b, i: (0, i)),
            pl.BlockSpec((A_ROPE, tm), lambda b, i: (0, i)),
            pl.BlockSpec((tm, LANES), lambda b, i: (i, 0)),
            pl.BlockSpec((tm, LANES), lambda b, i: (i, 0)),
        ],
        out_specs=[
            pl.BlockSpec((None, A_HEADS, None, A_QK_PAD, tm), lambda b, i: (b, 0, i, 0, 0)),
            pl.BlockSpec((None, A_HEADS, tm, A_QK_PAD), lambda b, i: (b, 0, i, 0)),
            pl.BlockSpec((None, A_HEADS, None, A_V, tm), lambda b, i: (b, 0, i, 0, 0)),
            pl.BlockSpec((None, tm, A_WIDTH), lambda b, i: (b, i, 0)),
        ],
        out_shape=[
            jax.ShapeDtypeStruct((bsz, A_HEADS, n_tiles, A_QK_PAD, tm), jnp.bfloat16),
            jax.ShapeDtypeStruct((bsz, A_HEADS, seq, A_QK_PAD), jnp.bfloat16),
            jax.ShapeDtypeStruct((bsz, A_HEADS, n_tiles, A_V, tm), jnp.bfloat16),
            jax.ShapeDtypeStruct((bsz, seq, A_WIDTH), jnp.float32),
        ],
        compiler_params=pltpu.CompilerParams(
            dimension_semantics=("parallel", "parallel"), vmem_limit_bytes=VMEM_LIMIT),
        name="mla_proj",
    )(x, g, w_in, qn, kvn, wq_t, wk, wv_t, cos_t, sin_t, cos_k, sin_k)


def _mla_attn_kernel(q_t_ref, k_ref, v_t_ref, sg_ref, o_ref, *scratch, exp2_scale):
    depth = ATTN_DEPTH
    s_bufs, p_bufs = scratch[0:depth], scratch[depth:2 * depth]
    cm_bufs, al_bufs = scratch[2 * depth:3 * depth], scratch[3 * depth:4 * depth]
    m_ref, l_ref, acc_ref = scratch[4 * depth:]
    n_tiles, _, tq = q_t_ref.shape
    n_chunks, _, tk = v_t_ref.shape
    total = n_tiles * n_chunks
    sub = m_ref.shape[0]

    m_ref[...] = jnp.zeros(m_ref.shape, jnp.float32)
    l_ref[...] = jnp.zeros(l_ref.shape, jnp.float32)
    acc_ref[...] = jnp.zeros(acc_ref.shape, jnp.float32)

    def stage_a(g, slot):
        tile, chunk = g // n_chunks, g % n_chunks
        k_c = k_ref[pl.ds(pl.multiple_of(chunk * tk, tk), tk), :]
        s = jnp.dot(k_c, q_t_ref[tile], preferred_element_type=jnp.float32)
        s_bufs[slot][...] = s
        cm_bufs[slot][...] = jnp.max(s.reshape(tk // sub, sub, tq), axis=0)

    def stage_b(g, slot):
        tile, chunk = g // n_chunks, g % n_chunks
        cm = cm_bufs[slot][...]
        for shift in (4, 2, 1):
            cm = jnp.maximum(cm, pltpu.roll(cm, shift, axis=0))
        m_old = jnp.where(chunk == 0, -jnp.inf, m_ref[...])
        m_new = jnp.maximum(m_old, cm)
        alpha = jnp.exp2((m_old - m_new) * exp2_scale)
        psum = None
        for r in range(tk // ATTN_RB):
            rows = slice(r * ATTN_RB, (r + 1) * ATTN_RB)
            s_blk = s_bufs[slot][rows, :].reshape(ATTN_RB // sub, sub, tq)
            p_blk = jnp.exp2((s_blk - m_new[None]) * exp2_scale)
            p_bufs[slot][rows, :] = p_blk.reshape(ATTN_RB, tq).astype(jnp.bfloat16)
            blk_sum = jnp.sum(p_blk, axis=0)
            psum = blk_sum if psum is None else psum + blk_sum
        l_ref[tile] = alpha * l_ref[tile] + psum
        al_bufs[slot][...] = alpha
        m_ref[...] = m_new

    def stage_c(g, slot):
        tile, chunk = g // n_chunks, g % n_chunks
        pv = jnp.dot(v_t_ref[chunk], p_bufs[slot][...], preferred_element_type=jnp.float32)
        acc = acc_ref[tile].reshape(A_V // sub, sub, tq) * al_bufs[slot][...][None]
        acc_ref[tile] = acc.reshape(A_V, tq) + pv

    def steps(tau0, static):
        for u in range(depth):
            g_b, g_c, g_a = tau0 + u - ATTN_B_LAG, tau0 + u - ATTN_C_LAG, tau0 + u
            if not static or 0 <= g_b < total:
                stage_b(g_b, (u - ATTN_B_LAG) % depth)
            if not static or 0 <= g_c < total:
                stage_c(g_c, (u - ATTN_C_LAG) % depth)
            if not static or 0 <= g_a < total:
                stage_a(g_a, u)

    steps(0, True)

    def body(i, carry):
        steps(i * depth, False)
        return carry

    lax.fori_loop(1, total // depth, body, 0)
    steps(total, True)

    def finish(tile, carry):
        rows = pl.ds(pl.multiple_of(tile * tq, tq), tq)
        l = jnp.sum(l_ref[tile], axis=0, keepdims=True)
        o = (acc_ref[tile] / l).T
        o_ref[rows, :] = (o * sg_ref[rows, :]).astype(o_ref.dtype)
        return carry

    lax.fori_loop(0, n_tiles, finish, 0)


def _mla_attn(q_t, k, v_t, sg):
    bsz, _, n_tiles, _, tq = q_t.shape
    seq = n_tiles * tq
    n_chunks = seq // ATTN_TK
    assert (n_tiles * n_chunks) % ATTN_DEPTH == 0 and n_tiles * n_chunks >= 2 * ATTN_DEPTH
    exp2_scale = (A_NOPE + A_ROPE) ** -0.5 * math.log2(math.e)
    f32, bf16 = jnp.float32, jnp.bfloat16
    return pl.pallas_call(
        functools.partial(_mla_attn_kernel, exp2_scale=exp2_scale),
        grid=(bsz, A_HEADS),
        in_specs=[
            pl.BlockSpec((None, None, n_tiles, A_QK_PAD, tq), lambda b, h: (b, h, 0, 0, 0)),
            pl.BlockSpec((None, None, seq, A_QK_PAD), lambda b, h: (b, h, 0, 0)),
            pl.BlockSpec((None, None, n_chunks, A_V, ATTN_TK), lambda b, h: (b, h, 0, 0, 0)),
            pl.BlockSpec((None, seq, A_V), lambda b, h: (b, 0, h)),
        ],
        out_specs=pl.BlockSpec((None, seq, A_V), lambda b, h: (b, 0, h)),
        out_shape=jax.ShapeDtypeStruct((bsz, seq, A_WIDTH), bf16),
        scratch_shapes=(
            [pltpu.VMEM((ATTN_TK, tq), f32) for _ in range(ATTN_DEPTH)]
            + [pltpu.VMEM((ATTN_TK, tq), bf16) for _ in range(ATTN_DEPTH)]
            + [pltpu.VMEM((SUBLANES, tq), f32) for _ in range(ATTN_DEPTH)]
            + [pltpu.VMEM((SUBLANES, tq), f32) for _ in range(ATTN_DEPTH)]
            + [pltpu.VMEM((SUBLANES, tq), f32),
               pltpu.VMEM((n_tiles, SUBLANES, tq), f32),
               pltpu.VMEM((n_tiles, A_V, tq), f32)]
        ),
        compiler_params=pltpu.CompilerParams(
            dimension_semantics=("parallel", "parallel"), vmem_limit_bytes=VMEM_LIMIT),
        name="mla_attn",
    )(q_t, k, v_t, sg)


def _gelu(x):
    return 0.5 * x * (1.0 + lax.erf(x * (1.0 / math.sqrt(2.0))))


def _gmlp_kernel(x_ref, og_ref, w_oa_ref, g_ref, w_in_ref, ln_g_ref, ln_b_ref, w_s_ref, b_s_ref,
                 w_out_ref, fg_ref, y_ref, s_ref, *, final_norm):
    tm = x_ref.shape[0]
    x1 = x_ref[...] + jnp.dot(og_ref[...], w_oa_ref[...], preferred_element_type=jnp.float32)
    hb = _rms(x1, g_ref[...]).astype(jnp.bfloat16)
    uvg = jnp.dot(hb, w_in_ref[...], preferred_element_type=jnp.float32)
    u = _gelu(uvg[:, :B_WIDTH])
    v = _gelu(uvg[:, B_WIDTH:2 * B_WIDTH])
    gate = uvg[:, 2 * B_WIDTH:]
    mu = jnp.mean(v, axis=-1, keepdims=True)
    vc = v - mu
    vn = (vc * lax.rsqrt(jnp.mean(vc * vc, axis=-1, keepdims=True) + LN_EPS)) * ln_g_ref[...] + ln_b_ref[...]
    vn = vn.astype(jnp.bfloat16)
    ug = u * (gate * jax.nn.sigmoid(gate))
    for c in range(tm // B_CHUNK):
        rows = slice(c * B_CHUNK, (c + 1) * B_CHUNK)
        for grp in range(B_GROUPS):
            cols = slice(grp * B_GROUP_DIM, (grp + 1) * B_GROUP_DIM)
            bias = b_s_ref[grp]
            sv = jnp.dot(w_s_ref[grp], vn[rows, cols], preferred_element_type=jnp.float32)
            sv = sv + jnp.concatenate([bias] * (B_GROUP_DIM // LANES), axis=1)
            s_ref[rows, cols] = (ug[rows, cols] * sv).astype(jnp.bfloat16)
    y = x1 + jnp.dot(s_ref[...], w_out_ref[...], preferred_element_type=jnp.float32)
    if final_norm:
        y = _rms(y, fg_ref[...])
    y_ref[...] = y


def _gmlp(x, og, w_oa, g, w_in, ln_g, ln_b, w_s, b_s, w_out, fg, *, final_norm):
    bsz, seq, _ = x.shape
    tm = GMLP_TM
    return pl.pallas_call(
        functools.partial(_gmlp_kernel, final_norm=final_norm),
        grid=(bsz, seq // tm),
        in_specs=[
            pl.BlockSpec((None, tm, D_MODEL), lambda b, i: (b, i, 0)),
            pl.BlockSpec((None, tm, A_WIDTH), lambda b, i: (b, i, 0)),
            _const_spec(w_oa.shape), _const_spec(g.shape), _const_spec(w_in.shape),
            _const_spec(ln_g.shape), _const_spec(ln_b.shape), _const_spec(w_s.shape),
            _const_spec(b_s.shape), _const_spec(w_out.shape), _const_spec(fg.shape),
        ],
        out_specs=pl.BlockSpec((None, tm, D_MODEL), lambda b, i: (b, i, 0)),
        out_shape=jax.ShapeDtypeStruct((bsz, seq, D_MODEL), jnp.float32),
        scratch_shapes=[pltpu.VMEM((tm, B_WIDTH), jnp.bfloat16)],
        compiler_params=pltpu.CompilerParams(
            dimension_semantics=("parallel", "parallel"), vmem_limit_bytes=VMEM_LIMIT),
        name="gmlp",
    )(x, og, w_oa, g, w_in, ln_g, ln_b, w_s, b_s, w_out, fg)


def _prep_mla(w_in, q_norm, kv_norm, w_q_up, w_kv_up, w_out):
    bf = jnp.bfloat16
    half = A_ROPE // 2
    c_kr = A_Q_LORA + A_KV_LORA
    w_kr = w_in[:, c_kr:c_kr + A_ROPE]
    w_kr_rot = jnp.concatenate([-w_kr[:, half:], w_kr[:, :half]], axis=1)
    zpad = jnp.zeros((D_MODEL, LANES - A_ROPE), w_in.dtype)
    w_in_p = jnp.concatenate(
        [w_in[:, :c_kr], w_in[:, c_kr + A_ROPE:], w_kr, zpad, w_kr_rot, zpad], axis=1).astype(bf)
    wq = w_q_up.reshape(A_Q_LORA, A_HEADS, A_NOPE + A_ROPE)
    wq_t = jnp.concatenate(
        [wq[:, :, :A_NOPE].reshape(A_Q_LORA, -1), wq[:, :, A_NOPE:].reshape(A_Q_LORA, -1)], axis=1).T.astype(bf)
    wkv = w_kv_up.reshape(A_KV_LORA, A_HEADS, A_NOPE + A_V)
    wk = wkv[:, :, :A_NOPE].reshape(A_KV_LORA, -1).astype(bf)
    wv_t = wkv[:, :, A_NOPE:].reshape(A_KV_LORA, -1).T.astype(bf)
    return (w_in_p, q_norm.reshape(1, -1), kv_norm.reshape(1, -1), wq_t, wk, wv_t, w_out.astype(bf))


def _prep_gmlp(w_in, ln_g, ln_b, w_s, b_s, w_out):
    bf = jnp.bfloat16
    b_rep = jnp.broadcast_to(b_s[:, :, None], (B_GROUPS, B_CHUNK, LANES))
    return (w_in.astype(bf), ln_g.reshape(1, -1), ln_b.reshape(1, -1), w_s.astype(bf), b_rep, w_out.astype(bf))


def _rope_tables(seq):
    inv = ROPE_THETA ** (-jnp.arange(0, A_ROPE, 2, dtype=jnp.float32) / A_ROPE)
    ang = jnp.arange(seq, dtype=jnp.float32)[:, None] * inv[None, :]
    ang = jnp.concatenate([ang, ang], axis=-1)
    cos, sin = jnp.cos(ang), jnp.sin(ang)
    half = A_ROPE // 2
    sign = jnp.concatenate([-jnp.ones((half,), jnp.float32), jnp.ones((half,), jnp.float32)])
    zpad = jnp.zeros((seq, LANES - A_ROPE), jnp.float32)
    cos_k = jnp.concatenate([cos, zpad], axis=1)
    sin_k = jnp.concatenate([sin, zpad], axis=1)
    return cos.T, (sin * sign).T, cos_k, sin_k


def _trunk(x, norm_g, final_g, mla_params, gmlp_params):
    seq = x.shape[1]
    tables = _rope_tables(seq)
    fg = final_g.reshape(1, -1)
    for j in range(DEPTH // 2):
        w_in_p, qn, kvn, wq_t, wk, wv_t, w_oa = mla_params[j]
        q_t, k, v_t, sg = _mla_proj(x, norm_g[2 * j].reshape(1, -1), w_in_p, qn, kvn, wq_t, wk, wv_t, *tables)
        og = _mla_attn(q_t, k, v_t, sg)
        x = _gmlp(x, og, w_oa, norm_g[2 * j + 1].reshape(1, -1), *gmlp_params[j], fg,
                  final_norm=(j == DEPTH // 2 - 1))
    return x


def kernel(x_prompt, x_sample, norm_g, final_g, a_w_in, a_q_norm, a_kv_norm, a_w_q_up, a_w_kv_up, a_w_out,
           b_w_in, b_ln_g, b_ln_b, b_w_s, b_b_s, b_w_out):
    mla_params = [_prep_mla(a_w_in[j], a_q_norm[j], a_kv_norm[j], a_w_q_up[j], a_w_kv_up[j], a_w_out[j])
                  for j in range(DEPTH // 2)]
    gmlp_params = [_prep_gmlp(b_w_in[j], b_ln_g[j], b_ln_b[j], b_w_s[j], b_b_s[j], b_w_out[j])
                   for j in range(DEPTH // 2)]
    y_prompt = _trunk(x_prompt, norm_g, final_g, mla_params, gmlp_params)
    y_sample = _trunk(x_sample, norm_g, final_g, mla_params, gmlp_params)
    return (y_prompt, y_sample)
```

```python
import functools
import math

import jax
import jax.numpy as jnp
from jax import lax
from jax.experimental import pallas as pl
from jax.experimental.pallas import tpu as pltpu

D_MODEL = 1024
DEPTH = 4

A_HEADS = 8
A_Q_LORA = 384
A_KV_LORA = 256
A_NOPE = 128
A_ROPE = 64
A_V = 128
A_WIDTH = A_HEADS * A_V
ROPE_THETA = 10000.0
A_QK_PAD = 256
A_V_AUG = A_V + 16

B_WIDTH = 2 * D_MODEL
B_GROUPS = 8
B_GROUP_DIM = B_WIDTH // B_GROUPS
B_CHUNK = 128

RMS_EPS = 1e-6
LN_EPS = 1e-5

LANES = 128
SUBLANES = 8
ATTN_TQ = 512
ATTN_TK = 512
ATTN_DEPTH = 4
ATTN_B_LAG = 2
ATTN_C_LAG = 3
ATTN_RB = 64
PROJ_TM = ATTN_TQ
GMLP_TM = 256
VMEM_LIMIT = 56 * 1024 * 1024

_C_Q = 0
_C_KV = _C_Q + A_Q_LORA
_C_GATE = _C_KV + A_KV_LORA
_C_KR = _C_GATE + A_WIDTH
_C_KRR = _C_KR + LANES
A_IN_PAD = _C_KRR + LANES

_NT = (((1,), (1,)), ((), ()))


def _const_spec(shape):
    return pl.BlockSpec(shape, lambda *_: (0,) * len(shape), pipeline_mode=pl.Buffered(1))


def _rms(x, g):
    return (x * lax.rsqrt(jnp.mean(x * x, axis=-1, keepdims=True) + RMS_EPS)) * g


def _mla_proj_kernel(x_ref, g_ref, w_in_ref, qn_ref, kvn_ref, wq_t_ref, wk_ref, wv_t_ref,
                     cos_t_ref, sin_t_ref, cos_k_ref, sin_k_ref,
                     q_t_ref, k_ref, v_t_ref, sg_ref):
    hb = _rms(x_ref[...], g_ref[...]).astype(jnp.bfloat16)
    proj = jnp.dot(hb, w_in_ref[...], preferred_element_type=jnp.float32)
    gate = proj[:, _C_GATE:_C_KR]
    sg_ref[...] = gate * jax.nn.sigmoid(gate)

    q_lat = _rms(proj[:, _C_Q:_C_KV], qn_ref[...]).astype(jnp.bfloat16)
    kv_lat = _rms(proj[:, _C_KV:_C_GATE], kvn_ref[...]).astype(jnp.bfloat16)

    kr = (proj[:, _C_KR:_C_KRR] * cos_k_ref[...]
          + proj[:, _C_KRR:A_IN_PAD] * sin_k_ref[...]).astype(jnp.bfloat16)

    q_t = lax.dot_general(wq_t_ref[...], q_lat, _NT, preferred_element_type=jnp.float32)
    k_nope = jnp.dot(kv_lat, wk_ref[...], preferred_element_type=jnp.float32)
    v_t = lax.dot_general(wv_t_ref[...], kv_lat, _NT, preferred_element_type=jnp.float32)

    cos_t = cos_t_ref[...]
    sin_t = sin_t_ref[...]
    half = A_ROPE // 2
    rope0 = A_HEADS * A_NOPE
    for h in range(A_HEADS):
        k_ref[h, :, 0:A_NOPE] = k_nope[:, h * A_NOPE:(h + 1) * A_NOPE].astype(jnp.bfloat16)
        k_ref[h, :, A_NOPE:A_QK_PAD] = kr
        q_t_ref[h, 0:A_NOPE, :] = q_t[h * A_NOPE:(h + 1) * A_NOPE, :].astype(jnp.bfloat16)
        qr = q_t[rope0 + h * A_ROPE: rope0 + (h + 1) * A_ROPE, :]
        rot = jnp.concatenate([qr[half:, :], qr[:half, :]], axis=0)
        q_t_ref[h, A_NOPE:A_NOPE + A_ROPE, :] = (qr * cos_t + rot * sin_t).astype(jnp.bfloat16)
        q_t_ref[h, A_NOPE + A_ROPE:A_QK_PAD, :] = jnp.zeros(
            (A_QK_PAD - A_NOPE - A_ROPE, q_t.shape[1]), jnp.bfloat16)
        v_t_ref[h, 0:A_V, :] = v_t[h * A_V:(h + 1) * A_V, :].astype(jnp.bfloat16)
        v_t_ref[h, A_V:A_V_AUG, :] = jnp.ones((A_V_AUG - A_V, v_t.shape[1]), jnp.bfloat16)


def _mla_proj(x, g, w_in, qn, kvn, wq_t, wk, wv_t, cos_t, sin_t, cos_k, sin_k):
    bsz, seq, _ = x.shape
    tm = PROJ_TM
    n_tiles = seq // tm
    return pl.pallas_call(
        _mla_proj_kernel,
        grid=(bsz, n_tiles),
        in_specs=[
            pl.BlockSpec((None, tm, D_MODEL), lambda b, i: (b, i, 0)),
            _const_spec(g.shape), _const_spec(w_in.shape), _const_spec(qn.shape), _const_spec(kvn.shape),
            _const_spec(wq_t.shape), _const_spec(wk.shape), _const_spec(wv_t.shape),
            pl.BlockSpec((A_ROPE, tm), lambda b, i: (0, i)),
            pl.BlockSpec((A_ROPE, tm), lambda b, i: (0, i)),
            pl.BlockSpec((tm, LANES), lambda b, i: (i, 0)),
            pl.BlockSpec((tm, LANES), lambda b, i: (i, 0)),
        ],
        out_specs=[
            pl.BlockSpec((None, A_HEADS, None, A_QK_PAD, tm), lambda b, i: (b, 0, i, 0, 0)),
            pl.BlockSpec((None, A_HEADS, tm, A_QK_PAD), lambda b, i: (b, 0, i, 0)),
            pl.BlockSpec((None, A_HEADS, None, A_V_AUG, tm), lambda b, i: (b, 0, i, 0, 0)),
            pl.BlockSpec((None, tm, A_WIDTH), lambda b, i: (b, i, 0)),
        ],
        out_shape=[
            jax.ShapeDtypeStruct((bsz, A_HEADS, n_tiles, A_QK_PAD, tm), jnp.bfloat16),
            jax.ShapeDtypeStruct((bsz, A_HEADS, seq, A_QK_PAD), jnp.bfloat16),
            jax.ShapeDtypeStruct((bsz, A_HEADS, n_tiles, A_V_AUG, tm), jnp.bfloat16),
            jax.ShapeDtypeStruct((bsz, seq, A_WIDTH), jnp.float32),
        ],
        compiler_params=pltpu.CompilerParams(
            dimension_semantics=("parallel", "parallel"), vmem_limit_bytes=VMEM_LIMIT),
        name="mla_proj",
    )(x, g, w_in, qn, kvn, wq_t, wk, wv_t, cos_t, sin_t, cos_k, sin_k)


def _mla_attn_kernel(q_t_ref, k_ref, v_t_ref, sg_ref, o_ref, *scratch, exp2_scale):
    depth = ATTN_DEPTH
    s_bufs, p_bufs = scratch[0:depth], scratch[depth:2 * depth]
    cm_bufs, al_bufs = scratch[2 * depth:3 * depth], scratch[3 * depth:4 * depth]
    m_ref, acc_ref = scratch[4 * depth:]
    n_tiles, _, tq = q_t_ref.shape
    n_chunks, v_rows, tk = v_t_ref.shape
    total = n_tiles * n_chunks
    sub = m_ref.shape[0]

    m_ref[...] = jnp.zeros(m_ref.shape, jnp.float32)
    acc_ref[...] = jnp.zeros(acc_ref.shape, jnp.float32)

    def stage_a(g, slot):
        tile, chunk = g // n_chunks, g % n_chunks
        k_c = k_ref[pl.ds(pl.multiple_of(chunk * tk, tk), tk), :]
        s = jnp.dot(k_c, q_t_ref[tile], preferred_element_type=jnp.float32)
        s_bufs[slot][...] = s
        cm_bufs[slot][...] = jnp.max(s.reshape(tk // sub, sub, tq), axis=0)

    def stage_b(g, slot):
        tile, chunk = g // n_chunks, g % n_chunks
        cm = cm_bufs[slot][...]
        for shift in (4, 2, 1):
            cm = jnp.maximum(cm, pltpu.roll(cm, shift, axis=0))
        m_old = jnp.where(chunk == 0, -jnp.inf, m_ref[...])
        m_new = jnp.maximum(m_old, cm)
        alpha = jnp.exp2((m_old - m_new) * exp2_scale)
        for r in range(tk // ATTN_RB):
            rows = slice(r * ATTN_RB, (r + 1) * ATTN_RB)
            s_blk = s_bufs[slot][rows, :].reshape(ATTN_RB // sub, sub, tq)
            p_blk = jnp.exp2((s_blk - m_new[None]) * exp2_scale)
            p_bufs[slot][rows, :] = p_blk.reshape(ATTN_RB, tq).astype(jnp.bfloat16)
        al_bufs[slot][...] = alpha
        m_ref[...] = m_new

    def stage_c(g, slot):
        tile, chunk = g // n_chunks, g % n_chunks
        pv = jnp.dot(v_t_ref[chunk], p_bufs[slot][...], preferred_element_type=jnp.float32)
        acc = acc_ref[tile].reshape(v_rows // sub, sub, tq) * al_bufs[slot][...][None]
        acc_ref[tile] = acc.reshape(v_rows, tq) + pv

    def steps(tau0, static):
        for u in range(depth):
            g_b, g_c, g_a = tau0 + u - ATTN_B_LAG, tau0 + u - ATTN_C_LAG, tau0 + u
            if not static or 0 <= g_b < total:
                stage_b(g_b, (u - ATTN_B_LAG) % depth)
            if not static or 0 <= g_c < total:
                stage_c(g_c, (u - ATTN_C_LAG) % depth)
            if not static or 0 <= g_a < total:
                stage_a(g_a, u)

    steps(0, True)

    def body(i, carry):
        steps(i * depth, False)
        return carry

    lax.fori_loop(1, total // depth, body, 0)
    steps(total, True)

    def finish(tile, carry):
        rows = pl.ds(pl.multiple_of(tile * tq, tq), tq)
        acc = acc_ref[tile]
        l = acc[A_V:A_V + 1, :]
        o = (acc[0:A_V, :] / l).T
        o_ref[rows, :] = (o * sg_ref[rows, :]).astype(o_ref.dtype)
        return carry

    lax.fori_loop(0, n_tiles, finish, 0)


def _mla_attn(q_t, k, v_t, sg):
    bsz, _, n_tiles, _, tq = q_t.shape
    seq = n_tiles * tq
    n_chunks = seq // ATTN_TK
    assert (n_tiles * n_chunks) % ATTN_DEPTH == 0 and n_tiles * n_chunks >= 2 * ATTN_DEPTH
    exp2_scale = (A_NOPE + A_ROPE) ** -0.5 * math.log2(math.e)
    f32, bf16 = jnp.float32, jnp.bfloat16
    return pl.pallas_call(
        functools.partial(_mla_attn_kernel, exp2_scale=exp2_scale),
        grid=(bsz, A_HEADS),
        in_specs=[
            pl.BlockSpec((None, None, n_tiles, A_QK_PAD, tq), lambda b, h: (b, h, 0, 0, 0)),
            pl.BlockSpec((None, None, seq, A_QK_PAD), lambda b, h: (b, h, 0, 0)),
            pl.BlockSpec((None, None, n_chunks, A_V_AUG, ATTN_TK), lambda b, h: (b, h, 0, 0, 0)),
            pl.BlockSpec((None, seq, A_V), lambda b, h: (b, 0, h)),
        ],
        out_specs=pl.BlockSpec((None, seq, A_V), lambda b, h: (b, 0, h)),
        out_shape=jax.ShapeDtypeStruct((bsz, seq, A_WIDTH), bf16),
        scratch_shapes=(
            [pltpu.VMEM((ATTN_TK, tq), f32) for _ in range(ATTN_DEPTH)]
            + [pltpu.VMEM((ATTN_TK, tq), bf16) for _ in range(ATTN_DEPTH)]
            + [pltpu.VMEM((SUBLANES, tq), f32) for _ in range(ATTN_DEPTH)]
            + [pltpu.VMEM((SUBLANES, tq), f32) for _ in range(ATTN_DEPTH)]
            + [pltpu.VMEM((SUBLANES, tq), f32),
               pltpu.VMEM((n_tiles, A_V_AUG, tq), f32)]
        ),
        compiler_params=pltpu.CompilerParams(
            dimension_semantics=("parallel", "parallel"), vmem_limit_bytes=VMEM_LIMIT),
        name="mla_attn",
    )(q_t, k, v_t, sg)


def _gelu(x):
    return 0.5 * x * (1.0 + lax.erf(x * (1.0 / math.sqrt(2.0))))


def _gmlp_kernel(x_ref, og_ref, w_oa_ref, g_ref, w_in_ref, ln_g_ref, ln_b_ref, w_s_ref, b_s_ref,
                 w_out_ref, fg_ref, y_ref, s_ref, *, final_norm):
    tm = x_ref.shape[0]
    x1 = x_ref[...] + jnp.dot(og_ref[...], w_oa_ref[...], preferred_element_type=jnp.float32)
    hb = _rms(x1, g_ref[...]).astype(jnp.bfloat16)
    uvg = jnp.dot(hb, w_in_ref[...], preferred_element_type=jnp.float32)
    u = _gelu(uvg[:, :B_WIDTH])
    v = _gelu(uvg[:, B_WIDTH:2 * B_WIDTH])
    gate = uvg[:, 2 * B_WIDTH:]
    mu = jnp.mean(v, axis=-1, keepdims=True)
    vc = v - mu
    vn = (vc * lax.rsqrt(jnp.mean(vc * vc, axis=-1, keepdims=True) + LN_EPS)) * ln_g_ref[...] + ln_b_ref[...]
    vn = vn.astype(jnp.bfloat16)
    ug = u * (gate * jax.nn.sigmoid(gate))
    for c in range(tm // B_CHUNK):
        rows = slice(c * B_CHUNK, (c + 1) * B_CHUNK)
        for grp in range(B_GROUPS):
            cols = slice(grp * B_GROUP_DIM, (grp + 1) * B_GROUP_DIM)
            bias = b_s_ref[grp]
            sv = jnp.dot(w_s_ref[grp], vn[rows, cols], preferred_element_type=jnp.float32)
            sv = sv + jnp.concatenate([bias] * (B_GROUP_DIM // LANES), axis=1)
            s_ref[rows, cols] = (ug[rows, cols] * sv).astype(jnp.bfloat16)
    y = x1 + jnp.dot(s_ref[...], w_out_ref[...], preferred_element_type=jnp.float32)
    if final_norm:
        y = _rms(y, fg_ref[...])
    y_ref[...] = y


def _gmlp(x, og, w_oa, g, w_in, ln_g, ln_b, w_s, b_s, w_out, fg, *, final_norm):
    bsz, seq, _ = x.shape
    tm = GMLP_TM
    return pl.pallas_call(
        functools.partial(_gmlp_kernel, final_norm=final_norm),
        grid=(bsz, seq // tm),
        in_specs=[
            pl.BlockSpec((None, tm, D_MODEL), lambda b, i: (b, i, 0)),
            pl.BlockSpec((None, tm, A_WIDTH), lambda b, i: (b, i, 0)),
            _const_spec(w_oa.shape), _const_spec(g.shape), _const_spec(w_in.shape),
            _const_spec(ln_g.shape), _const_spec(ln_b.shape), _const_spec(w_s.shape),
            _const_spec(b_s.shape), _const_spec(w_out.shape), _const_spec(fg.shape),
        ],
        out_specs=pl.BlockSpec((None, tm, D_MODEL), lambda b, i: (b, i, 0)),
        out_shape=jax.ShapeDtypeStruct((bsz, seq, D_MODEL), jnp.float32),
        scratch_shapes=[pltpu.VMEM((tm, B_WIDTH), jnp.bfloat16)],
        compiler_params=pltpu.CompilerParams(
            dimension_semantics=("parallel", "parallel"), vmem_limit_bytes=VMEM_LIMIT),
        name="gmlp",
    )(x, og, w_oa, g, w_in, ln_g, ln_b, w_s, b_s, w_out, fg)


def _prep_mla(w_in, q_norm, kv_norm, w_q_up, w_kv_up, w_out):
    bf = jnp.bfloat16
    half = A_ROPE // 2
    c_kr = A_Q_LORA + A_KV_LORA
    w_kr = w_in[:, c_kr:c_kr + A_ROPE]
    w_kr_rot = jnp.concatenate([-w_kr[:, half:], w_kr[:, :half]], axis=1)
    zpad = jnp.zeros((D_MODEL, LANES - A_ROPE), w_in.dtype)
    w_in_p = jnp.concatenate(
        [w_in[:, :c_kr], w_in[:, c_kr + A_ROPE:], w_kr, zpad, w_kr_rot, zpad], axis=1).astype(bf)
    wq = w_q_up.reshape(A_Q_LORA, A_HEADS, A_NOPE + A_ROPE)
    wq_t = jnp.concatenate(
        [wq[:, :, :A_NOPE].reshape(A_Q_LORA, -1), wq[:, :, A_NOPE:].reshape(A_Q_LORA, -1)], axis=1).T.astype(bf)
    wkv = w_kv_up.reshape(A_KV_LORA, A_HEADS, A_NOPE + A_V)
    wk = wkv[:, :, :A_NOPE].reshape(A_KV_LORA, -1).astype(bf)
    wv_t = wkv[:, :, A_NOPE:].reshape(A_KV_LORA, -1).T.astype(bf)
    return (w_in_p, q_norm.reshape(1, -1), kv_norm.reshape(1, -1), wq_t, wk, wv_t, w_out.astype(bf))


def _prep_gmlp(w_in, ln_g, ln_b, w_s, b_s, w_out):
    bf = jnp.bfloat16
    b_rep = jnp.broadcast_to(b_s[:, :, None], (B_GROUPS, B_CHUNK, LANES))
    return (w_in.astype(bf), ln_g.reshape(1, -1), ln_b.reshape(1, -1), w_s.astype(bf), b_rep, w_out.astype(bf))


def _rope_tables(seq):
    inv = ROPE_THETA ** (-jnp.arange(0, A_ROPE, 2, dtype=jnp.float32) / A_ROPE)
    ang = jnp.arange(seq, dtype=jnp.float32)[:, None] * inv[None, :]
    ang = jnp.concatenate([ang, ang], axis=-1)
    cos, sin = jnp.cos(ang), jnp.sin(ang)
    half = A_ROPE // 2
    sign = jnp.concatenate([-jnp.ones((half,), jnp.float32), jnp.ones((half,), jnp.float32)])
    zpad = jnp.zeros((seq, LANES - A_ROPE), jnp.float32)
    cos_k = jnp.concatenate([cos, zpad], axis=1)
    sin_k = jnp.concatenate([sin, zpad], axis=1)
    return cos.T, (sin * sign).T, cos_k, sin_k


def _trunk(x, norm_g, final_g, mla_params, gmlp_params):
    seq = x.shape[1]
    tables = _rope_tables(seq)
    fg = final_g.reshape(1, -1)
    for j in range(DEPTH // 2):
        w_in_p, qn, kvn, wq_t, wk, wv_t, w_oa = mla_params[j]
        q_t, k, v_t, sg = _mla_proj(x, norm_g[2 * j].reshape(1, -1), w_in_p, qn, kvn, wq_t, wk, wv_t, *tables)
        og = _mla_attn(q_t, k, v_t, sg)
        x = _gmlp(x, og, w_oa, norm_g[2 * j + 1].reshape(1, -1), *gmlp_params[j], fg,
                  final_norm=(j == DEPTH // 2 - 1))
    return x


def kernel(x_prompt, x_sample, norm_g, final_g, a_w_in, a_q_norm, a_kv_norm, a_w_q_up, a_w_kv_up, a_w_out,
           b_w_in, b_ln_g, b_ln_b, b_w_s, b_b_s, b_w_out):
    mla_params = [_prep_mla(a_w_in[j], a_q_norm[j], a_kv_norm[j], a_w_q_up[j], a_w_kv_up[j], a_w_out[j])
                  for j in range(DEPTH // 2)]
    gmlp_params = [_prep_gmlp(b_w_in[j], b_ln_g[j], b_ln_b[j], b_w_s[j], b_b_s[j], b_w_out[j])
                   for j in range(DEPTH // 2)]
    y_prompt = _trunk(x_prompt, norm_g, final_g, mla_params, gmlp_params)
    y_sample = _trunk(x_sample, norm_g, final_g, mla_params, gmlp_params)
    return (y_prompt, y_sample)
```

```python
import functools
import math

import jax
import jax.numpy as jnp
from jax import lax
from jax.experimental import pallas as pl
from jax.experimental.pallas import tpu as pltpu

D_MODEL = 1024
DEPTH = 4

A_HEADS = 8
A_Q_LORA = 384
A_KV_LORA = 256
A_NOPE = 128
A_ROPE = 64
A_V = 128
A_WIDTH = A_HEADS * A_V
ROPE_THETA = 10000.0
A_QK_PAD = 256
A_V_AUG = A_V + 16

B_WIDTH = 2 * D_MODEL
B_GROUPS = 8
B_GROUP_DIM = B_WIDTH // B_GROUPS
B_CHUNK = 128

RMS_EPS = 1e-6
LN_EPS = 1e-5

LANES = 128
SUBLANES = 8
ATTN_TQ = 512
ATTN_TK = 1024
ATTN_DEPTH = 4
ATTN_B_LAG = 2
ATTN_C_LAG = 3
ATTN_RB = 64
PROJ_TM = ATTN_TQ
GMLP_TM = 256
VMEM_LIMIT = 56 * 1024 * 1024

_C_Q = 0
_C_KV = _C_Q + A_Q_LORA
_C_GATE = _C_KV + A_KV_LORA
_C_KR = _C_GATE + A_WIDTH
_C_KRR = _C_KR + LANES
A_IN_PAD = _C_KRR + LANES

_NT = (((1,), (1,)), ((), ()))


def _const_spec(shape):
    return pl.BlockSpec(shape, lambda *_: (0,) * len(shape), pipeline_mode=pl.Buffered(1))


def _rms(x, g):
    return (x * lax.rsqrt(jnp.mean(x * x, axis=-1, keepdims=True) + RMS_EPS)) * g


def _mla_proj_kernel(x_ref, g_ref, w_in_ref, qn_ref, kvn_ref, wq_t_ref, wk_ref, wv_t_ref,
                     cos_t_ref, sin_t_ref, cos_k_ref, sin_k_ref,
                     q_t_ref, k_ref, v_t_ref, sg_ref):
    hb = _rms(x_ref[...], g_ref[...]).astype(jnp.bfloat16)
    proj = jnp.dot(hb, w_in_ref[...], preferred_element_type=jnp.float32)
    gate = proj[:, _C_GATE:_C_KR]
    sg_ref[...] = gate * jax.nn.sigmoid(gate)

    q_lat = _rms(proj[:, _C_Q:_C_KV], qn_ref[...]).astype(jnp.bfloat16)
    kv_lat = _rms(proj[:, _C_KV:_C_GATE], kvn_ref[...]).astype(jnp.bfloat16)

    kr = (proj[:, _C_KR:_C_KRR] * cos_k_ref[...]
          + proj[:, _C_KRR:A_IN_PAD] * sin_k_ref[...]).astype(jnp.bfloat16)

    q_t = lax.dot_general(wq_t_ref[...], q_lat, _NT, preferred_element_type=jnp.float32)
    k_nope = jnp.dot(kv_lat, wk_ref[...], preferred_element_type=jnp.float32)
    v_t = lax.dot_general(wv_t_ref[...], kv_lat, _NT, preferred_element_type=jnp.float32)

    cos_t = cos_t_ref[...]
    sin_t = sin_t_ref[...]
    half = A_ROPE // 2
    rope0 = A_HEADS * A_NOPE
    for h in range(A_HEADS):
        k_ref[h, :, 0:A_NOPE] = k_nope[:, h * A_NOPE:(h + 1) * A_NOPE].astype(jnp.bfloat16)
        k_ref[h, :, A_NOPE:A_QK_PAD] = kr
        q_t_ref[h, 0:A_NOPE, :] = q_t[h * A_NOPE:(h + 1) * A_NOPE, :].astype(jnp.bfloat16)
        qr = q_t[rope0 + h * A_ROPE: rope0 + (h + 1) * A_ROPE, :]
        rot = jnp.concatenate([qr[half:, :], qr[:half, :]], axis=0)
        q_t_ref[h, A_NOPE:A_NOPE + A_ROPE, :] = (qr * cos_t + rot * sin_t).astype(jnp.bfloat16)
        q_t_ref[h, A_NOPE + A_ROPE:A_QK_PAD, :] = jnp.zeros(
            (A_QK_PAD - A_NOPE - A_ROPE, q_t.shape[1]), jnp.bfloat16)
        v_t_ref[h, 0:A_V, :] = v_t[h * A_V:(h + 1) * A_V, :].astype(jnp.bfloat16)
        v_t_ref[h, A_V:A_V_AUG, :] = jnp.ones((A_V_AUG - A_V, v_t.shape[1]), jnp.bfloat16)


def _mla_proj(x, g, w_in, qn, kvn, wq_t, wk, wv_t, cos_t, sin_t, cos_k, sin_k):
    bsz, seq, _ = x.shape
    tm = PROJ_TM
    n_tiles = seq // tm
    tiles_per_chunk = ATTN_TK // tm
    return pl.pallas_call(
        _mla_proj_kernel,
        grid=(bsz, n_tiles),
        in_specs=[
            pl.BlockSpec((None, tm, D_MODEL), lambda b, i: (b, i, 0)),
            _const_spec(g.shape), _const_spec(w_in.shape), _const_spec(qn.shape), _const_spec(kvn.shape),
            _const_spec(wq_t.shape), _const_spec(wk.shape), _const_spec(wv_t.shape),
            pl.BlockSpec((A_ROPE, tm), lambda b, i: (0, i)),
            pl.BlockSpec((A_ROPE, tm), lambda b, i: (0, i)),
            pl.BlockSpec((tm, LANES), lambda b, i: (i, 0)),
            pl.BlockSpec((tm, LANES), lambda b, i: (i, 0)),
        ],
        out_specs=[
            pl.BlockSpec((None, A_HEADS, None, A_QK_PAD, tm), lambda b, i: (b, 0, i, 0, 0)),
            pl.BlockSpec((None, A_HEADS, tm, A_QK_PAD), lambda b, i: (b, 0, i, 0)),
            pl.BlockSpec((None, A_HEADS, None, A_V_AUG, tm), lambda b, i: (b, 0, i // tiles_per_chunk, 0, i % tiles_per_chunk)),
            pl.BlockSpec((None, tm, A_WIDTH), lambda b, i: (b, i, 0)),
        ],
        out_shape=[
            jax.ShapeDtypeStruct((bsz, A_HEADS, n_tiles, A_QK_PAD, tm), jnp.bfloat16),
            jax.ShapeDtypeStruct((bsz, A_HEADS, seq, A_QK_PAD), jnp.bfloat16),
            jax.ShapeDtypeStruct((bsz, A_HEADS, seq // ATTN_TK, A_V_AUG, ATTN_TK), jnp.bfloat16),
            jax.ShapeDtypeStruct((bsz, seq, A_WIDTH), jnp.float32),
        ],
        compiler_params=pltpu.CompilerParams(
            dimension_semantics=("parallel", "parallel"), vmem_limit_bytes=VMEM_LIMIT),
        name="mla_proj",
    )(x, g, w_in, qn, kvn, wq_t, wk, wv_t, cos_t, sin_t, cos_k, sin_k)


def _mla_attn_kernel(q_t_ref, k_ref, v_t_ref, sg_ref, o_ref, *scratch, exp2_scale):
    depth = ATTN_DEPTH
    s_bufs, p_bufs = scratch[0:depth], scratch[depth:2 * depth]
    cm_bufs, al_bufs = scratch[2 * depth:3 * depth], scratch[3 * depth:4 * depth]
    m_ref, acc_ref = scratch[4 * depth:]
    n_tiles, _, tq = q_t_ref.shape
    n_chunks, v_rows, tk = v_t_ref.shape
    total = n_tiles * n_chunks
    sub = m_ref.shape[0]

    m_ref[...] = jnp.zeros(m_ref.shape, jnp.float32)
    acc_ref[...] = jnp.zeros(acc_ref.shape, jnp.float32)

    def stage_a(g, slot):
        tile, chunk = g // n_chunks, g % n_chunks
        k_c = k_ref[pl.ds(pl.multiple_of(chunk * tk, tk), tk), :]
        s = jnp.dot(k_c, q_t_ref[tile], preferred_element_type=jnp.float32)
        s_bufs[slot][...] = s
        cm_bufs[slot][...] = jnp.max(s.reshape(tk // sub, sub, tq), axis=0)

    def stage_b(g, slot):
        tile, chunk = g // n_chunks, g % n_chunks
        cm = cm_bufs[slot][...]
        for shift in (4, 2, 1):
            cm = jnp.maximum(cm, pltpu.roll(cm, shift, axis=0))
        m_old = jnp.where(chunk == 0, -jnp.inf, m_ref[...])
        m_new = jnp.maximum(m_old, cm)
        alpha = jnp.exp2((m_old - m_new) * exp2_scale)
        for r in range(tk // ATTN_RB):
            rows = slice(r * ATTN_RB, (r + 1) * ATTN_RB)
            s_blk = s_bufs[slot][rows, :].reshape(ATTN_RB // sub, sub, tq)
            p_blk = jnp.exp2((s_blk - m_new[None]) * exp2_scale)
            p_bufs[slot][rows, :] = p_blk.reshape(ATTN_RB, tq).astype(jnp.bfloat16)
        al_bufs[slot][...] = alpha
        m_ref[...] = m_new

    def stage_c(g, slot):
        tile, chunk = g // n_chunks, g % n_chunks
        pv = jnp.dot(v_t_ref[chunk], p_bufs[slot][...], preferred_element_type=jnp.float32)
        acc = acc_ref[tile].reshape(v_rows // sub, sub, tq) * al_bufs[slot][...][None]
        acc_ref[tile] = acc.reshape(v_rows, tq) + pv

    def steps(tau0, static):
        for u in range(depth):
            g_b, g_c, g_a = tau0 + u - ATTN_B_LAG, tau0 + u - ATTN_C_LAG, tau0 + u
            if not static or 0 <= g_b < total:
                stage_b(g_b, (u - ATTN_B_LAG) % depth)
            if not static or 0 <= g_c < total:
                stage_c(g_c, (u - ATTN_C_LAG) % depth)
            if not static or 0 <= g_a < total:
                stage_a(g_a, u)

    steps(0, True)

    def body(i, carry):
        steps(i * depth, False)
        return carry

    lax.fori_loop(1, total // depth, body, 0)
    steps(total, True)

    def finish(tile, carry):
        rows = pl.ds(pl.multiple_of(tile * tq, tq), tq)
        acc = acc_ref[tile]
        l = acc[A_V:A_V + 1, :]
        o = (acc[0:A_V, :] / l).T
        o_ref[rows, :] = (o * sg_ref[rows, :]).astype(o_ref.dtype)
        return carry

    lax.fori_loop(0, n_tiles, finish, 0)


def _mla_attn(q_t, k, v_t, sg):
    bsz, _, n_tiles, _, tq = q_t.shape
    seq = n_tiles * tq
    n_chunks = seq // ATTN_TK
    assert (n_tiles * n_chunks) % ATTN_DEPTH == 0 and n_tiles * n_chunks >= 2 * ATTN_DEPTH
    exp2_scale = (A_NOPE + A_ROPE) ** -0.5 * math.log2(math.e)
    f32, bf16 = jnp.float32, jnp.bfloat16
    return pl.pallas_call(
        functools.partial(_mla_attn_kernel, exp2_scale=exp2_scale),
        grid=(bsz, A_HEADS),
        in_specs=[
            pl.BlockSpec((None, None, n_tiles, A_QK_PAD, tq), lambda b, h: (b, h, 0, 0, 0)),
            pl.BlockSpec((None, None, seq, A_QK_PAD), lambda b, h: (b, h, 0, 0)),
            pl.BlockSpec((None, None, n_chunks, A_V_AUG, ATTN_TK), lambda b, h: (b, h, 0, 0, 0)),
            pl.BlockSpec((None, seq, A_V), lambda b, h: (b, 0, h)),
        ],
        out_specs=pl.BlockSpec((None, seq, A_V), lambda b, h: (b, 0, h)),
        out_shape=jax.ShapeDtypeStruct((bsz, seq, A_WIDTH), bf16),
        scratch_shapes=(
            [pltpu.VMEM((ATTN_TK, tq), f32) for _ in range(ATTN_DEPTH)]
            + [pltpu.VMEM((ATTN_TK, tq), bf16) for _ in range(ATTN_DEPTH)]
            + [pltpu.VMEM((SUBLANES, tq), f32) for _ in range(ATTN_DEPTH)]
            + [pltpu.VMEM((SUBLANES, tq), f32) for _ in range(ATTN_DEPTH)]
            + [pltpu.VMEM((SUBLANES, tq), f32),
               pltpu.VMEM((n_tiles, A_V_AUG, tq), f32)]
        ),
        compiler_params=pltpu.CompilerParams(
            dimension_semantics=("parallel", "parallel"), vmem_limit_bytes=VMEM_LIMIT),
        name="mla_attn",
    )(q_t, k, v_t, sg)


def _gelu(x):
    return 0.5 * x * (1.0 + lax.erf(x * (1.0 / math.sqrt(2.0))))


def _gmlp_kernel(x_ref, og_ref, w_oa_ref, g_ref, w_in_ref, ln_g_ref, ln_b_ref, w_s_ref, b_s_ref,
                 w_out_ref, fg_ref, y_ref, s_ref, *, final_norm):
    tm = x_ref.shape[0]
    x1 = x_ref[...] + jnp.dot(og_ref[...], w_oa_ref[...], preferred_element_type=jnp.float32)
    hb = _rms(x1, g_ref[...]).astype(jnp.bfloat16)
    uvg = jnp.dot(hb, w_in_ref[...], preferred_element_type=jnp.float32)
    u = _gelu(uvg[:, :B_WIDTH])
    v = _gelu(uvg[:, B_WIDTH:2 * B_WIDTH])
    gate = uvg[:, 2 * B_WIDTH:]
    mu = jnp.mean(v, axis=-1, keepdims=True)
    vc = v - mu
    vn = (vc * lax.rsqrt(jnp.mean(vc * vc, axis=-1, keepdims=True) + LN_EPS)) * ln_g_ref[...] + ln_b_ref[...]
    vn = vn.astype(jnp.bfloat16)
    ug = u * (gate * jax.nn.sigmoid(gate))
    for c in range(tm // B_CHUNK):
        rows = slice(c * B_CHUNK, (c + 1) * B_CHUNK)
        for grp in range(B_GROUPS):
            cols = slice(grp * B_GROUP_DIM, (grp + 1) * B_GROUP_DIM)
            bias = b_s_ref[grp]
            sv = jnp.dot(w_s_ref[grp], vn[rows, cols], preferred_element_type=jnp.float32)
            sv = sv + jnp.concatenate([bias] * (B_GROUP_DIM // LANES), axis=1)
            s_ref[rows, cols] = (ug[rows, cols] * sv).astype(jnp.bfloat16)
    y = x1 + jnp.dot(s_ref[...], w_out_ref[...], preferred_element_type=jnp.float32)
    if final_norm:
        y = _rms(y, fg_ref[...])
    y_ref[...] = y


def _gmlp(x, og, w_oa, g, w_in, ln_g, ln_b, w_s, b_s, w_out, fg, *, final_norm):
    bsz, seq, _ = x.shape
    tm = GMLP_TM
    return pl.pallas_call(
        functools.partial(_gmlp_kernel, final_norm=final_norm),
        grid=(bsz, seq // tm),
        in_specs=[
            pl.BlockSpec((None, tm, D_MODEL), lambda b, i: (b, i, 0)),
            pl.BlockSpec((None, tm, A_WIDTH), lambda b, i: (b, i, 0)),
            _const_spec(w_oa.shape), _const_spec(g.shape), _const_spec(w_in.shape),
            _const_spec(ln_g.shape), _const_spec(ln_b.shape), _const_spec(w_s.shape),
            _const_spec(b_s.shape), _const_spec(w_out.shape), _const_spec(fg.shape),
        ],
        out_specs=pl.BlockSpec((None, tm, D_MODEL), lambda b, i: (b, i, 0)),
        out_shape=jax.ShapeDtypeStruct((bsz, seq, D_MODEL), jnp.float32),
        scratch_shapes=[pltpu.VMEM((tm, B_WIDTH), jnp.bfloat16)],
        compiler_params=pltpu.CompilerParams(
            dimension_semantics=("parallel", "parallel"), vmem_limit_bytes=VMEM_LIMIT),
        name="gmlp",
    )(x, og, w_oa, g, w_in, ln_g, ln_b, w_s, b_s, w_out, fg)


def _prep_mla(w_in, q_norm, kv_norm, w_q_up, w_kv_up, w_out):
    bf = jnp.bfloat16
    half = A_ROPE // 2
    c_kr = A_Q_LORA + A_KV_LORA
    w_kr = w_in[:, c_kr:c_kr + A_ROPE]
    w_kr_rot = jnp.concatenate([-w_kr[:, half:], w_kr[:, :half]], axis=1)
    zpad = jnp.zeros((D_MODEL, LANES - A_ROPE), w_in.dtype)
    w_in_p = jnp.concatenate(
        [w_in[:, :c_kr], w_in[:, c_kr + A_ROPE:], w_kr, zpad, w_kr_rot, zpad], axis=1).astype(bf)
    wq = w_q_up.reshape(A_Q_LORA, A_HEADS, A_NOPE + A_ROPE)
    wq_t = jnp.concatenate(
        [wq[:, :, :A_NOPE].reshape(A_Q_LORA, -1), wq[:, :, A_NOPE:].reshape(A_Q_LORA, -1)], axis=1).T.astype(bf)
    wkv = w_kv_up.reshape(A_KV_LORA, A_HEADS, A_NOPE + A_V)
    wk = wkv[:, :, :A_NOPE].reshape(A_KV_LORA, -1).astype(bf)
    wv_t = wkv[:, :, A_NOPE:].reshape(A_KV_LORA, -1).T.astype(bf)
    return (w_in_p, q_norm.reshape(1, -1), kv_norm.reshape(1, -1), wq_t, wk, wv_t, w_out.astype(bf))


def _prep_gmlp(w_in, ln_g, ln_b, w_s, b_s, w_out):
    bf = jnp.bfloat16
    b_rep = jnp.broadcast_to(b_s[:, :, None], (B_GROUPS, B_CHUNK, LANES))
    return (w_in.astype(bf), ln_g.reshape(1, -1), ln_b.reshape(1, -1), w_s.astype(bf), b_rep, w_out.astype(bf))


def _rope_tables(seq):
    inv = ROPE_THETA ** (-jnp.arange(0, A_ROPE, 2, dtype=jnp.float32) / A_ROPE)
    ang = jnp.arange(seq, dtype=jnp.float32)[:, None] * inv[None, :]
    ang = jnp.concatenate([ang, ang], axis=-1)
    cos, sin = jnp.cos(ang), jnp.sin(ang)
    half = A_ROPE // 2
    sign = jnp.concatenate([-jnp.ones((half,), jnp.float32), jnp.ones((half,), jnp.float32)])
    zpad = jnp.zeros((seq, LANES - A_ROPE), jnp.float32)
    cos_k = jnp.concatenate([cos, zpad], axis=1)
    sin_k = jnp.concatenate([sin, zpad], axis=1)
    return cos.T, (sin * sign).T, cos_k, sin_k


def _trunk(x, norm_g, final_g, mla_params, gmlp_params):
    seq = x.shape[1]
    tables = _rope_tables(seq)
    fg = final_g.reshape(1, -1)
    for j in range(DEPTH // 2):
        w_in_p, qn, kvn, wq_t, wk, wv_t, w_oa = mla_params[j]
        q_t, k, v_t, sg = _mla_proj(x, norm_g[2 * j].reshape(1, -1), w_in_p, qn, kvn, wq_t, wk, wv_t, *tables)
        og = _mla_attn(q_t, k, v_t, sg)
        x = _gmlp(x, og, w_oa, norm_g[2 * j + 1].reshape(1, -1), *gmlp_params[j], fg,
                  final_norm=(j == DEPTH // 2 - 1))
    return x


def kernel(x_prompt, x_sample, norm_g, final_g, a_w_in, a_q_norm, a_kv_norm, a_w_q_up, a_w_kv_up, a_w_out,
           b_w_in, b_ln_g, b_ln_b, b_w_s, b_b_s, b_w_out):
    mla_params = [_prep_mla(a_w_in[j], a_q_norm[j], a_kv_norm[j], a_w_q_up[j], a_w_kv_up[j], a_w_out[j])
                  for j in range(DEPTH // 2)]
    gmlp_params = [_prep_gmlp(b_w_in[j], b_ln_g[j], b_ln_b[j], b_w_s[j], b_b_s[j], b_w_out[j])
                   for j in range(DEPTH // 2)]
    y_prompt = _trunk(x_prompt, norm_g, final_g, mla_params, gmlp_params)
    y_sample = _trunk(x_sample, norm_g, final_g, mla_params, gmlp_params)
    return (y_prompt, y_sample)
```

```python
import functools
import math

import jax
import jax.numpy as jnp
from jax import lax
from jax.experimental import pallas as pl
from jax.experimental.pallas import tpu as pltpu

D_MODEL = 1024
DEPTH = 4

A_HEADS = 8
A_Q_LORA = 384
A_KV_LORA = 256
A_NOPE = 128
A_ROPE = 64
A_V = 128
A_WIDTH = A_HEADS * A_V
ROPE_THETA = 10000.0
A_QK_PAD = 256
A_V_AUG = A_V + 16

B_WIDTH = 2 * D_MODEL
B_GROUPS = 8
B_GROUP_DIM = B_WIDTH // B_GROUPS
B_CHUNK = 128

RMS_EPS = 1e-6
LN_EPS = 1e-5

LANES = 128
SUBLANES = 8
ATTN_TQ = 512
ATTN_TK = 1024
ATTN_DEPTH = 4
ATTN_B_LAG = 2
ATTN_C_LAG = 3
ATTN_RB = 64
PROJ_TM = ATTN_TQ
GMLP_TM = 512
VMEM_LIMIT = 56 * 1024 * 1024

_C_Q = 0
_C_KV = _C_Q + A_Q_LORA
_C_GATE = _C_KV + A_KV_LORA
_C_KR = _C_GATE + A_WIDTH
_C_KRR = _C_KR + LANES
A_IN_PAD = _C_KRR + LANES

_NT = (((1,), (1,)), ((), ()))

EXP2_SCALE = (A_NOPE + A_ROPE) ** -0.5 * math.log2(math.e)


def _const_spec(shape):
    return pl.BlockSpec(shape, lambda *_: (0,) * len(shape), pipeline_mode=pl.Buffered(1))


def _rms(x, g):
    return (x * lax.rsqrt(jnp.mean(x * x, axis=-1, keepdims=True) + RMS_EPS)) * g


def _mla_proj_kernel(x_ref, g_ref, w_in_ref, qn_ref, kvn_ref, wq_t_ref, wk_ref, wv_t_ref,
                     cos_t_ref, sin_t_ref, cos_k_ref, sin_k_ref,
                     q_t_ref, k_ref, v_t_ref, sg_ref):
    hb = _rms(x_ref[...], g_ref[...]).astype(jnp.bfloat16)
    proj = jnp.dot(hb, w_in_ref[...], preferred_element_type=jnp.float32)
    gate = proj[:, _C_GATE:_C_KR]
    sg_ref[...] = gate * jax.nn.sigmoid(gate)

    q_lat = _rms(proj[:, _C_Q:_C_KV], qn_ref[...]).astype(jnp.bfloat16)
    kv_lat = _rms(proj[:, _C_KV:_C_GATE], kvn_ref[...]).astype(jnp.bfloat16)

    kr = (proj[:, _C_KR:_C_KRR] * cos_k_ref[...]
          + proj[:, _C_KRR:A_IN_PAD] * sin_k_ref[...]).astype(jnp.bfloat16)

    q_t = lax.dot_general(wq_t_ref[...], q_lat, _NT, preferred_element_type=jnp.float32)
    k_nope = jnp.dot(kv_lat, wk_ref[...], preferred_element_type=jnp.float32)
    v_t = lax.dot_general(wv_t_ref[...], kv_lat, _NT, preferred_element_type=jnp.float32)

    cos_t = cos_t_ref[...]
    sin_t = sin_t_ref[...]
    half = A_ROPE // 2
    rope0 = A_HEADS * A_NOPE
    for h in range(A_HEADS):
        k_ref[h, :, 0:A_NOPE] = k_nope[:, h * A_NOPE:(h + 1) * A_NOPE].astype(jnp.bfloat16)
        k_ref[h, :, A_NOPE:A_QK_PAD] = kr
        q_t_ref[h, 0:A_NOPE, :] = (q_t[h * A_NOPE:(h + 1) * A_NOPE, :] * EXP2_SCALE).astype(jnp.bfloat16)
        qr = q_t[rope0 + h * A_ROPE: rope0 + (h + 1) * A_ROPE, :]
        rot = jnp.concatenate([qr[half:, :], qr[:half, :]], axis=0)
        q_t_ref[h, A_NOPE:A_NOPE + A_ROPE, :] = ((qr * cos_t + rot * sin_t) * EXP2_SCALE).astype(jnp.bfloat16)
        q_t_ref[h, A_NOPE + A_ROPE:A_QK_PAD, :] = jnp.zeros(
            (A_QK_PAD - A_NOPE - A_ROPE, q_t.shape[1]), jnp.bfloat16)
        v_t_ref[h, 0:A_V, :] = v_t[h * A_V:(h + 1) * A_V, :].astype(jnp.bfloat16)
        v_t_ref[h, A_V:A_V_AUG, :] = jnp.ones((A_V_AUG - A_V, v_t.shape[1]), jnp.bfloat16)


def _mla_proj(x, g, w_in, qn, kvn, wq_t, wk, wv_t, cos_t, sin_t, cos_k, sin_k):
    bsz, seq, _ = x.shape
    tm = PROJ_TM
    n_tiles = seq // tm
    tiles_per_chunk = ATTN_TK // tm
    return pl.pallas_call(
        _mla_proj_kernel,
        grid=(bsz, n_tiles),
        in_specs=[
            pl.BlockSpec((None, tm, D_MODEL), lambda b, i: (b, i, 0)),
            _const_spec(g.shape), _const_spec(w_in.shape), _const_spec(qn.shape), _const_spec(kvn.shape),
            _const_spec(wq_t.shape), _const_spec(wk.shape), _const_spec(wv_t.shape),
            pl.BlockSpec((A_ROPE, tm), lambda b, i: (0, i)),
            pl.BlockSpec((A_ROPE, tm), lambda b, i: (0, i)),
            pl.BlockSpec((tm, LANES), lambda b, i: (i, 0)),
            pl.BlockSpec((tm, LANES), lambda b, i: (i, 0)),
        ],
        out_specs=[
            pl.BlockSpec((None, A_HEADS, None, A_QK_PAD, tm), lambda b, i: (b, 0, i, 0, 0)),
            pl.BlockSpec((None, A_HEADS, tm, A_QK_PAD), lambda b, i: (b, 0, i, 0)),
            pl.BlockSpec((None, A_HEADS, None, A_V_AUG, tm), lambda b, i: (b, 0, i // tiles_per_chunk, 0, i % tiles_per_chunk)),
            pl.BlockSpec((None, tm, A_WIDTH), lambda b, i: (b, i, 0)),
        ],
        out_shape=[
            jax.ShapeDtypeStruct((bsz, A_HEADS, n_tiles, A_QK_PAD, tm), jnp.bfloat16),
            jax.ShapeDtypeStruct((bsz, A_HEADS, seq, A_QK_PAD), jnp.bfloat16),
            jax.ShapeDtypeStruct((bsz, A_HEADS, seq // ATTN_TK, A_V_AUG, ATTN_TK), jnp.bfloat16),
            jax.ShapeDtypeStruct((bsz, seq, A_WIDTH), jnp.float32),
        ],
        compiler_params=pltpu.CompilerParams(
            dimension_semantics=("parallel", "parallel"), vmem_limit_bytes=VMEM_LIMIT),
        name="mla_proj",
    )(x, g, w_in, qn, kvn, wq_t, wk, wv_t, cos_t, sin_t, cos_k, sin_k)


def _mla_attn_kernel(q_t_ref, k_ref, v_t_ref, sg_ref, o_ref, *scratch):
    depth = ATTN_DEPTH
    s_bufs, p_bufs = scratch[0:depth], scratch[depth:2 * depth]
    cm_bufs, al_bufs = scratch[2 * depth:3 * depth], scratch[3 * depth:4 * depth]
    m_ref, acc_ref = scratch[4 * depth:]
    n_tiles, _, tq = q_t_ref.shape
    n_chunks, v_rows, tk = v_t_ref.shape
    total = n_tiles * n_chunks
    sub = m_ref.shape[0]

    m_ref[...] = jnp.zeros(m_ref.shape, jnp.float32)
    acc_ref[...] = jnp.zeros(acc_ref.shape, jnp.float32)

    def stage_a(g, slot):
        tile, chunk = g // n_chunks, g % n_chunks
        k_c = k_ref[pl.ds(pl.multiple_of(chunk * tk, tk), tk), :]
        s = jnp.dot(k_c, q_t_ref[tile], preferred_element_type=jnp.float32)
        s_bufs[slot][...] = s
        cm_bufs[slot][...] = jnp.max(s.reshape(tk // sub, sub, tq), axis=0)

    def stage_b(g, slot):
        tile, chunk = g // n_chunks, g % n_chunks
        cm = cm_bufs[slot][...]
        for shift in (4, 2, 1):
            cm = jnp.maximum(cm, pltpu.roll(cm, shift, axis=0))
        m_old = jnp.where(chunk == 0, -jnp.inf, m_ref[...])
        m_new = jnp.maximum(m_old, cm)
        alpha = jnp.exp2(m_old - m_new)
        for r in range(tk // ATTN_RB):
            rows = slice(r * ATTN_RB, (r + 1) * ATTN_RB)
            s_blk = s_bufs[slot][rows, :].reshape(ATTN_RB // sub, sub, tq)
            p_blk = jnp.exp2(s_blk - m_new[None])
            p_bufs[slot][rows, :] = p_blk.reshape(ATTN_RB, tq).astype(jnp.bfloat16)
        al_bufs[slot][...] = alpha
        m_ref[...] = m_new

    def stage_c(g, slot):
        tile, chunk = g // n_chunks, g % n_chunks
        pv = jnp.dot(v_t_ref[chunk], p_bufs[slot][...], preferred_element_type=jnp.float32)
        acc = acc_ref[tile].reshape(v_rows // sub, sub, tq) * al_bufs[slot][...][None]
        acc_ref[tile] = acc.reshape(v_rows, tq) + pv

    def steps(tau0, static):
        for u in range(depth):
            g_b, g_c, g_a = tau0 + u - ATTN_B_LAG, tau0 + u - ATTN_C_LAG, tau0 + u
            if not static or 0 <= g_b < total:
                stage_b(g_b, (u - ATTN_B_LAG) % depth)
            if not static or 0 <= g_c < total:
                stage_c(g_c, (u - ATTN_C_LAG) % depth)
            if not static or 0 <= g_a < total:
                stage_a(g_a, u)

    steps(0, True)

    def body(i, carry):
        steps(i * depth, False)
        return carry

    lax.fori_loop(1, total // depth, body, 0)
    steps(total, True)

    def finish(tile, carry):
        rows = pl.ds(pl.multiple_of(tile * tq, tq), tq)
        acc = acc_ref[tile]
        l = acc[A_V:A_V + 1, :]
        o = (acc[0:A_V, :] / l).T
        o_ref[rows, :] = (o * sg_ref[rows, :]).astype(o_ref.dtype)
        return carry

    lax.fori_loop(0, n_tiles, finish, 0)


def _mla_attn(q_t, k, v_t, sg):
    bsz, _, n_tiles, _, tq = q_t.shape
    seq = n_tiles * tq
    n_chunks = seq // ATTN_TK
    assert (n_tiles * n_chunks) % ATTN_DEPTH == 0 and n_tiles * n_chunks >= 2 * ATTN_DEPTH
    f32, bf16 = jnp.float32, jnp.bfloat16
    return pl.pallas_call(
        _mla_attn_kernel,
        grid=(bsz, A_HEADS),
        in_specs=[
            pl.BlockSpec((None, None, n_tiles, A_QK_PAD, tq), lambda b, h: (b, h, 0, 0, 0)),
            pl.BlockSpec((None, None, seq, A_QK_PAD), lambda b, h: (b, h, 0, 0)),
            pl.BlockSpec((None, None, n_chunks, A_V_AUG, ATTN_TK), lambda b, h: (b, h, 0, 0, 0)),
            pl.BlockSpec((None, seq, A_V), lambda b, h: (b, 0, h)),
        ],
        out_specs=pl.BlockSpec((None, seq, A_V), lambda b, h: (b, 0, h)),
        out_shape=jax.ShapeDtypeStruct((bsz, seq, A_WIDTH), bf16),
        scratch_shapes=(
            [pltpu.VMEM((ATTN_TK, tq), f32) for _ in range(ATTN_DEPTH)]
            + [pltpu.VMEM((ATTN_TK, tq), bf16) for _ in range(ATTN_DEPTH)]
            + [pltpu.VMEM((SUBLANES, tq), f32) for _ in range(ATTN_DEPTH)]
            + [pltpu.VMEM((SUBLANES, tq), f32) for _ in range(ATTN_DEPTH)]
            + [pltpu.VMEM((SUBLANES, tq), f32),
               pltpu.VMEM((n_tiles, A_V_AUG, tq), f32)]
        ),
        compiler_params=pltpu.CompilerParams(
            dimension_semantics=("parallel", "parallel"), vmem_limit_bytes=VMEM_LIMIT),
        name="mla_attn",
    )(q_t, k, v_t, sg)


def _gelu(x):
    return 0.5 * x * (1.0 + lax.erf(x * (1.0 / math.sqrt(2.0))))


def _gmlp_kernel(x_ref, og_ref, w_oa_ref, g_ref, w_in_ref, ln_g_ref, ln_b_ref, w_s_ref, b_s_ref,
                 w_out_ref, fg_ref, y_ref, s_ref, *, final_norm):
    tm = x_ref.shape[0]
    x1 = x_ref[...] + jnp.dot(og_ref[...], w_oa_ref[...], preferred_element_type=jnp.float32)
    hb = _rms(x1, g_ref[...]).astype(jnp.bfloat16)
    uvg = jnp.dot(hb, w_in_ref[...], preferred_element_type=jnp.float32)
    u = _gelu(uvg[:, :B_WIDTH])
    v = _gelu(uvg[:, B_WIDTH:2 * B_WIDTH])
    gate = uvg[:, 2 * B_WIDTH:]
    mu = jnp.mean(v, axis=-1, keepdims=True)
    vc = v - mu
    vn = (vc * lax.rsqrt(jnp.mean(vc * vc, axis=-1, keepdims=True) + LN_EPS)) * ln_g_ref[...] + ln_b_ref[...]
    vn = vn.astype(jnp.bfloat16)
    ug = u * (gate * jax.nn.sigmoid(gate))
    for c in range(tm // B_CHUNK):
        rows = slice(c * B_CHUNK, (c + 1) * B_CHUNK)
        for grp in range(B_GROUPS):
            cols = slice(grp * B_GROUP_DIM, (grp + 1) * B_GROUP_DIM)
            bias = b_s_ref[grp]
            sv = jnp.dot(w_s_ref[grp], vn[rows, cols], preferred_element_type=jnp.float32)
            sv = sv + jnp.concatenate([bias] * (B_GROUP_DIM // LANES), axis=1)
            s_ref[rows, cols] = (ug[rows, cols] * sv).astype(jnp.bfloat16)
    y = x1 + jnp.dot(s_ref[...], w_out_ref[...], preferred_element_type=jnp.float32)
    if final_norm:
        y = _rms(y, fg_ref[...])
    y_ref[...] = y


def _gmlp(x, og, w_oa, g, w_in, ln_g, ln_b, w_s, b_s, w_out, fg, *, final_norm):
    bsz, seq, _ = x.shape
    tm = GMLP_TM
    return pl.pallas_call(
        functools.partial(_gmlp_kernel, final_norm=final_norm),
        grid=(bsz, seq // tm),
        in_specs=[
            pl.BlockSpec((None, tm, D_MODEL), lambda b, i: (b, i, 0)),
            pl.BlockSpec((None, tm, A_WIDTH), lambda b, i: (b, i, 0)),
            _const_spec(w_oa.shape), _const_spec(g.shape), _const_spec(w_in.shape),
            _const_spec(ln_g.shape), _const_spec(ln_b.shape), _const_spec(w_s.shape),
            _const_spec(b_s.shape), _const_spec(w_out.shape), _const_spec(fg.shape),
        ],
        out_specs=pl.BlockSpec((None, tm, D_MODEL), lambda b, i: (b, i, 0)),
        out_shape=jax.ShapeDtypeStruct((bsz, seq, D_MODEL), jnp.float32),
        scratch_shapes=[pltpu.VMEM((tm, B_WIDTH), jnp.bfloat16)],
        compiler_params=pltpu.CompilerParams(
            dimension_semantics=("parallel", "parallel"), vmem_limit_bytes=VMEM_LIMIT),
        name="gmlp",
    )(x, og, w_oa, g, w_in, ln_g, ln_b, w_s, b_s, w_out, fg)


def _prep_mla(w_in, q_norm, kv_norm, w_q_up, w_kv_up, w_out):
    bf = jnp.bfloat16
    half = A_ROPE // 2
    c_kr = A_Q_LORA + A_KV_LORA
    w_kr = w_in[:, c_kr:c_kr + A_ROPE]
    w_kr_rot = jnp.concatenate([-w_kr[:, half:], w_kr[:, :half]], axis=1)
    zpad = jnp.zeros((D_MODEL, LANES - A_ROPE), w_in.dtype)
    w_in_p = jnp.concatenate(
        [w_in[:, :c_kr], w_in[:, c_kr + A_ROPE:], w_kr, zpad, w_kr_rot, zpad], axis=1).astype(bf)
    wq = w_q_up.reshape(A_Q_LORA, A_HEADS, A_NOPE + A_ROPE)
    wq_t = jnp.concatenate(
        [wq[:, :, :A_NOPE].reshape(A_Q_LORA, -1), wq[:, :, A_NOPE:].reshape(A_Q_LORA, -1)], axis=1).T.astype(bf)
    wkv = w_kv_up.reshape(A_KV_LORA, A_HEADS, A_NOPE + A_V)
    wk = wkv[:, :, :A_NOPE].reshape(A_KV_LORA, -1).astype(bf)
    wv_t = wkv[:, :, A_NOPE:].reshape(A_KV_LORA, -1).T.astype(bf)
    return (w_in_p, q_norm.reshape(1, -1), kv_norm.reshape(1, -1), wq_t, wk, wv_t, w_out.astype(bf))


def _prep_gmlp(w_in, ln_g, ln_b, w_s, b_s, w_out):
    bf = jnp.bfloat16
    b_rep = jnp.broadcast_to(b_s[:, :, None], (B_GROUPS, B_CHUNK, LANES))
    return (w_in.astype(bf), ln_g.reshape(1, -1), ln_b.reshape(1, -1), w_s.astype(bf), b_rep, w_out.astype(bf))


def _rope_tables(seq):
    inv = ROPE_THETA ** (-jnp.arange(0, A_ROPE, 2, dtype=jnp.float32) / A_ROPE)
    ang = jnp.arange(seq, dtype=jnp.float32)[:, None] * inv[None, :]
    ang = jnp.concatenate([ang, ang], axis=-1)
    cos, sin = jnp.cos(ang), jnp.sin(ang)
    half = A_ROPE // 2
    sign = jnp.concatenate([-jnp.ones((half,), jnp.float32), jnp.ones((half,), jnp.float32)])
    zpad = jnp.zeros((seq, LANES - A_ROPE), jnp.float32)
    cos_k = jnp.concatenate([cos, zpad], axis=1)
    sin_k = jnp.concatenate([sin, zpad], axis=1)
    return cos.T, (sin * sign).T, cos_k, sin_k


def _trunk(x, norm_g, final_g, mla_params, gmlp_params):
    seq = x.shape[1]
    tables = _rope_tables(seq)
    fg = final_g.reshape(1, -1)
    for j in range(DEPTH // 2):
        w_in_p, qn, kvn, wq_t, wk, wv_t, w_oa = mla_params[j]
        q_t, k, v_t, sg = _mla_proj(x, norm_g[2 * j].reshape(1, -1), w_in_p, qn, kvn, wq_t, wk, wv_t, *tables)
        og = _mla_attn(q_t, k, v_t, sg)
        x = _gmlp(x, og, w_oa, norm_g[2 * j + 1].reshape(1, -1), *gmlp_params[j], fg,
                  final_norm=(j == DEPTH // 2 - 1))
    return x


def kernel(x_prompt, x_sample, norm_g, final_g, a_w_in, a_q_norm, a_kv_norm, a_w_q_up, a_w_kv_up, a_w_out,
           b_w_in, b_ln_g, b_ln_b, b_w_s, b_b_s, b_w_out):
    mla_params = [_prep_mla(a_w_in[j], a_q_norm[j], a_kv_norm[j], a_w_q_up[j], a_w_kv_up[j], a_w_out[j])
                  for j in range(DEPTH // 2)]
    gmlp_params = [_prep_gmlp(b_w_in[j], b_ln_g[j], b_ln_b[j], b_w_s[j], b_b_s[j], b_w_out[j])
                   for j in range(DEPTH // 2)]
    y_prompt = _trunk(x_prompt, norm_g, final_g, mla_params, gmlp_params)
    y_sample = _trunk(x_sample, norm_g, final_g, mla_params, gmlp_params)
    return (y_prompt, y_sample)
```

```python
import functools
import math

import jax
import jax.numpy as jnp
from jax import lax
from jax.experimental import pallas as pl
from jax.experimental.pallas import tpu as pltpu

D_MODEL = 1024
DEPTH = 4

A_HEADS = 8
A_Q_LORA = 384
A_KV_LORA = 256
A_NOPE = 128
A_ROPE = 64
A_V = 128
A_WIDTH = A_HEADS * A_V
ROPE_THETA = 10000.0
A_QK_PAD = 256
A_V_AUG = A_V + 16

B_WIDTH = 2 * D_MODEL
B_GROUPS = 8
B_GROUP_DIM = B_WIDTH // B_GROUPS
B_CHUNK = 128

RMS_EPS = 1e-6
LN_EPS = 1e-5

LANES = 128
SUBLANES = 8
ATTN_TQ = 512
ATTN_TK = 1024
ATTN_DEPTH = 4
ATTN_B_LAG = 2
ATTN_C_LAG = 3
ATTN_RB = 64
PROJ_TM = ATTN_TQ
GMLP_TM = 512
VMEM_LIMIT = 56 * 1024 * 1024
ATTN_VMEM_BUDGET = 52 * 1024 * 1024

_C_Q = 0
_C_KV = _C_Q + A_Q_LORA
_C_GATE = _C_KV + A_KV_LORA
_C_KR = _C_GATE + A_WIDTH
_C_KRR = _C_KR + LANES
A_IN_PAD = _C_KRR + LANES

_NT = (((1,), (1,)), ((), ()))

EXP2_SCALE = (A_NOPE + A_ROPE) ** -0.5 * math.log2(math.e)


def _const_spec(shape):
    return pl.BlockSpec(shape, lambda *_: (0,) * len(shape), pipeline_mode=pl.Buffered(1))


def _rms(x, g):
    return (x * lax.rsqrt(jnp.mean(x * x, axis=-1, keepdims=True) + RMS_EPS)) * g


def _mla_proj_kernel(x_ref, g_ref, w_in_ref, qn_ref, kvn_ref, wq_t_ref, wk_ref, wv_t_ref,
                     cos_t_ref, sin_t_ref, cos_k_ref, sin_k_ref,
                     q_t_ref, k_ref, v_t_ref, sg_ref):
    hb = _rms(x_ref[...], g_ref[...]).astype(jnp.bfloat16)
    proj = jnp.dot(hb, w_in_ref[...], preferred_element_type=jnp.float32)
    gate = proj[:, _C_GATE:_C_KR]
    sg = gate * jax.nn.sigmoid(gate)

    q_lat = _rms(proj[:, _C_Q:_C_KV], qn_ref[...]).astype(jnp.bfloat16)
    kv_lat = _rms(proj[:, _C_KV:_C_GATE], kvn_ref[...]).astype(jnp.bfloat16)

    kr = (proj[:, _C_KR:_C_KRR] * cos_k_ref[...]
          + proj[:, _C_KRR:A_IN_PAD] * sin_k_ref[...]).astype(jnp.bfloat16)

    q_t = lax.dot_general(wq_t_ref[...], q_lat, _NT, preferred_element_type=jnp.float32)
    k_nope = jnp.dot(kv_lat, wk_ref[...], preferred_element_type=jnp.float32)
    v_t = lax.dot_general(wv_t_ref[...], kv_lat, _NT, preferred_element_type=jnp.float32)

    cos_t = cos_t_ref[...]
    sin_t = sin_t_ref[...]
    half = A_ROPE // 2
    rope0 = A_HEADS * A_NOPE
    for h in range(A_HEADS):
        k_ref[h, :, 0:A_NOPE] = k_nope[:, h * A_NOPE:(h + 1) * A_NOPE].astype(jnp.bfloat16)
        k_ref[h, :, A_NOPE:A_QK_PAD] = kr
        q_t_ref[h, 0:A_NOPE, :] = (q_t[h * A_NOPE:(h + 1) * A_NOPE, :] * EXP2_SCALE).astype(jnp.bfloat16)
        qr = q_t[rope0 + h * A_ROPE: rope0 + (h + 1) * A_ROPE, :]
        rot = jnp.concatenate([qr[half:, :], qr[:half, :]], axis=0)
        q_t_ref[h, A_NOPE:A_NOPE + A_ROPE, :] = ((qr * cos_t + rot * sin_t) * EXP2_SCALE).astype(jnp.bfloat16)
        q_t_ref[h, A_NOPE + A_ROPE:A_QK_PAD, :] = jnp.zeros(
            (A_QK_PAD - A_NOPE - A_ROPE, q_t.shape[1]), jnp.bfloat16)
        v_t_ref[h, 0:A_V, :] = v_t[h * A_V:(h + 1) * A_V, :].astype(jnp.bfloat16)
        sg_ref[h] = sg[:, h * A_V:(h + 1) * A_V]
        v_t_ref[h, A_V:A_V_AUG, :] = jnp.ones((A_V_AUG - A_V, v_t.shape[1]), jnp.bfloat16)


def _mla_proj(x, g, w_in, qn, kvn, wq_t, wk, wv_t, cos_t, sin_t, cos_k, sin_k):
    bsz, seq, _ = x.shape
    tm = PROJ_TM
    n_tiles = seq // tm
    tiles_per_chunk = ATTN_TK // tm
    return pl.pallas_call(
        _mla_proj_kernel,
        grid=(bsz, n_tiles),
        in_specs=[
            pl.BlockSpec((None, tm, D_MODEL), lambda b, i: (b, i, 0)),
            _const_spec(g.shape), _const_spec(w_in.shape), _const_spec(qn.shape), _const_spec(kvn.shape),
            _const_spec(wq_t.shape), _const_spec(wk.shape), _const_spec(wv_t.shape),
            pl.BlockSpec((A_ROPE, tm), lambda b, i: (0, i)),
            pl.BlockSpec((A_ROPE, tm), lambda b, i: (0, i)),
            pl.BlockSpec((tm, LANES), lambda b, i: (i, 0)),
            pl.BlockSpec((tm, LANES), lambda b, i: (i, 0)),
        ],
        out_specs=[
            pl.BlockSpec((None, A_HEADS, None, A_QK_PAD, tm), lambda b, i: (b, 0, i, 0, 0)),
            pl.BlockSpec((None, A_HEADS, tm, A_QK_PAD), lambda b, i: (b, 0, i, 0)),
            pl.BlockSpec((None, A_HEADS, None, A_V_AUG, tm), lambda b, i: (b, 0, i // tiles_per_chunk, 0, i % tiles_per_chunk)),
            pl.BlockSpec((None, A_HEADS, tm, A_V), lambda b, i: (b, 0, i, 0)),
        ],
        out_shape=[
            jax.ShapeDtypeStruct((bsz, A_HEADS, n_tiles, A_QK_PAD, tm), jnp.bfloat16),
            jax.ShapeDtypeStruct((bsz, A_HEADS, seq, A_QK_PAD), jnp.bfloat16),
            jax.ShapeDtypeStruct((bsz, A_HEADS, seq // ATTN_TK, A_V_AUG, ATTN_TK), jnp.bfloat16),
            jax.ShapeDtypeStruct((bsz, A_HEADS, seq, A_V), jnp.float32),
        ],
        compiler_params=pltpu.CompilerParams(
            dimension_semantics=("parallel", "parallel"), vmem_limit_bytes=VMEM_LIMIT),
        name="mla_proj",
    )(x, g, w_in, qn, kvn, wq_t, wk, wv_t, cos_t, sin_t, cos_k, sin_k)


def _mla_attn_kernel(q_t_ref, k_ref, v_t_ref, sg_ref, o_ref, *scratch):
    depth = ATTN_DEPTH
    s_bufs, p_bufs = scratch[0:depth], scratch[depth:2 * depth]
    cm_bufs, al_bufs = scratch[2 * depth:3 * depth], scratch[3 * depth:4 * depth]
    m_ref, acc_ref = scratch[4 * depth:]
    n_heads, n_tiles, _, tq = q_t_ref.shape
    _, n_chunks, v_rows, tk = v_t_ref.shape
    total = n_heads * n_tiles * n_chunks
    finish_in_loop = n_chunks == depth
    sub = m_ref.shape[0]

    def split(g):
        gt = g // n_chunks
        return gt // n_tiles, gt % n_tiles, g % n_chunks, gt

    m_ref[...] = jnp.zeros(m_ref.shape, jnp.float32)
    acc_ref[...] = jnp.zeros(acc_ref.shape, jnp.float32)

    def stage_a(g, slot):
        head, tile, chunk, _ = split(g)
        k_c = k_ref[head, pl.ds(pl.multiple_of(chunk * tk, tk), tk), :]
        s = jnp.dot(k_c, q_t_ref[head, tile], preferred_element_type=jnp.float32)
        s_bufs[slot][...] = s
        cm_bufs[slot][...] = jnp.max(s.reshape(tk // sub, sub, tq), axis=0)

    def stage_b(g, slot):
        _, _, chunk, _ = split(g)
        cm = cm_bufs[slot][...]
        for shift in (4, 2, 1):
            cm = jnp.maximum(cm, pltpu.roll(cm, shift, axis=0))
        m_old = jnp.where(chunk == 0, -jnp.inf, m_ref[...])
        m_new = jnp.maximum(m_old, cm)
        alpha = jnp.exp2(m_old - m_new)
        for r in range(tk // ATTN_RB):
            rows = slice(r * ATTN_RB, (r + 1) * ATTN_RB)
            s_blk = s_bufs[slot][rows, :].reshape(ATTN_RB // sub, sub, tq)
            p_blk = jnp.exp2(s_blk - m_new[None])
            p_bufs[slot][rows, :] = p_blk.reshape(ATTN_RB, tq).astype(jnp.bfloat16)
        al_bufs[slot][...] = alpha
        m_ref[...] = m_new

    def stage_c(g, slot):
        head, _, chunk, gt = split(g)
        pv = jnp.dot(v_t_ref[head, chunk], p_bufs[slot][...], preferred_element_type=jnp.float32)
        acc = acc_ref[gt].reshape(v_rows // sub, sub, tq) * al_bufs[slot][...][None]
        acc_ref[gt] = acc.reshape(v_rows, tq) + pv

    def finish(gt):
        head, tile = gt // n_tiles, gt % n_tiles
        rows = pl.ds(pl.multiple_of(tile * tq, tq), tq)
        acc = acc_ref[gt]
        l = acc[A_V:A_V + 1, :]
        o = (acc[0:A_V, :] / l).T
        o_ref[head, rows, :] = (o * sg_ref[head, rows, :]).astype(o_ref.dtype)

    def steps(tau0, static):
        for u in range(depth):
            g_b, g_c, g_a = tau0 + u - ATTN_B_LAG, tau0 + u - ATTN_C_LAG, tau0 + u
            if not static or 0 <= g_b < total:
                stage_b(g_b, (u - ATTN_B_LAG) % depth)
            if not static or 0 <= g_c < total:
                stage_c(g_c, (u - ATTN_C_LAG) % depth)
            if not static or 0 <= g_a < total:
                stage_a(g_a, u)
            if finish_in_loop and u == ATTN_C_LAG - 1 and (not static or tau0 >= depth):
                finish(tau0 // depth - 1)

    steps(0, True)

    def body(i, carry):
        steps(i * depth, False)
        return carry

    lax.fori_loop(1, total // depth, body, 0)
    steps(total, True)

    if not finish_in_loop:
        def finish_body(gt, carry):
            finish(gt)
            return carry

        lax.fori_loop(0, n_heads * n_tiles, finish_body, 0)


def _heads_per_step(seq):
    per_head = 2 * seq * (2 * A_QK_PAD * 2 + A_V_AUG * 2 + A_V * 4 + A_V * 2) + seq * A_V_AUG * 4
    fixed = ATTN_DEPTH * ATTN_TK * ATTN_TQ * (4 + 2)
    hps = 1
    while hps < A_HEADS and 2 * hps * per_head + fixed <= ATTN_VMEM_BUDGET:
        hps *= 2
    return hps


def _mla_attn(q_t, k, v_t, sg):
    bsz, _, n_tiles, _, tq = q_t.shape
    seq = n_tiles * tq
    n_chunks = seq // ATTN_TK
    hps = _heads_per_step(seq)
    assert (hps * n_tiles * n_chunks) % ATTN_DEPTH == 0 and hps * n_tiles * n_chunks >= 2 * ATTN_DEPTH
    f32, bf16 = jnp.float32, jnp.bfloat16
    return pl.pallas_call(
        _mla_attn_kernel,
        grid=(bsz, A_HEADS // hps),
        in_specs=[
            pl.BlockSpec((None, hps, n_tiles, A_QK_PAD, tq), lambda b, h: (b, h, 0, 0, 0)),
            pl.BlockSpec((None, hps, seq, A_QK_PAD), lambda b, h: (b, h, 0, 0)),
            pl.BlockSpec((None, hps, n_chunks, A_V_AUG, ATTN_TK), lambda b, h: (b, h, 0, 0, 0)),
            pl.BlockSpec((None, hps, seq, A_V), lambda b, h: (b, h, 0, 0)),
        ],
        out_specs=pl.BlockSpec((None, hps, seq, A_V), lambda b, h: (b, h, 0, 0)),
        out_shape=jax.ShapeDtypeStruct((bsz, A_HEADS, seq, A_V), bf16),
        scratch_shapes=(
            [pltpu.VMEM((ATTN_TK, tq), f32) for _ in range(ATTN_DEPTH)]
            + [pltpu.VMEM((ATTN_TK, tq), bf16) for _ in range(ATTN_DEPTH)]
            + [pltpu.VMEM((SUBLANES, tq), f32) for _ in range(ATTN_DEPTH)]
            + [pltpu.VMEM((SUBLANES, tq), f32) for _ in range(ATTN_DEPTH)]
            + [pltpu.VMEM((SUBLANES, tq), f32),
               pltpu.VMEM((hps * n_tiles, A_V_AUG, tq), f32)]
        ),
        compiler_params=pltpu.CompilerParams(
            dimension_semantics=("parallel", "parallel"), vmem_limit_bytes=VMEM_LIMIT),
        name="mla_attn",
    )(q_t, k, v_t, sg)


def _gelu(x):
    return 0.5 * x * (1.0 + lax.erf(x * (1.0 / math.sqrt(2.0))))


def _gmlp_kernel(x_ref, og_ref, w_oa_ref, g_ref, w_in_ref, ln_g_ref, ln_b_ref, w_s_ref, b_s_ref,
                 w_out_ref, fg_ref, y_ref, s_ref, *, final_norm):
    tm = x_ref.shape[0]
    og = jnp.concatenate([og_ref[h] for h in range(A_HEADS)], axis=1)
    x1 = x_ref[...] + jnp.dot(og, w_oa_ref[...], preferred_element_type=jnp.float32)
    hb = _rms(x1, g_ref[...]).astype(jnp.bfloat16)
    uvg = jnp.dot(hb, w_in_ref[...], preferred_element_type=jnp.float32)
    u = _gelu(uvg[:, :B_WIDTH])
    v = _gelu(uvg[:, B_WIDTH:2 * B_WIDTH])
    gate = uvg[:, 2 * B_WIDTH:]
    mu = jnp.mean(v, axis=-1, keepdims=True)
    vc = v - mu
    vn = (vc * lax.rsqrt(jnp.mean(vc * vc, axis=-1, keepdims=True) + LN_EPS)) * ln_g_ref[...] + ln_b_ref[...]
    vn = vn.astype(jnp.bfloat16)
    ug = u * (gate * jax.nn.sigmoid(gate))
    for c in range(tm // B_CHUNK):
        rows = slice(c * B_CHUNK, (c + 1) * B_CHUNK)
        for grp in range(B_GROUPS):
            cols = slice(grp * B_GROUP_DIM, (grp + 1) * B_GROUP_DIM)
            bias = b_s_ref[grp]
            sv = jnp.dot(w_s_ref[grp], vn[rows, cols], preferred_element_type=jnp.float32)
            sv = sv + jnp.concatenate([bias] * (B_GROUP_DIM // LANES), axis=1)
            s_ref[rows, cols] = (ug[rows, cols] * sv).astype(jnp.bfloat16)
    y = x1 + jnp.dot(s_ref[...], w_out_ref[...], preferred_element_type=jnp.float32)
    if final_norm:
        y = _rms(y, fg_ref[...])
    y_ref[...] = y


def _gmlp(x, og, w_oa, g, w_in, ln_g, ln_b, w_s, b_s, w_out, fg, *, final_norm):
    bsz, seq, _ = x.shape
    tm = GMLP_TM
    return pl.pallas_call(
        functools.partial(_gmlp_kernel, final_norm=final_norm),
        grid=(bsz, seq // tm),
        in_specs=[
            pl.BlockSpec((None, tm, D_MODEL), lambda b, i: (b, i, 0)),
            pl.BlockSpec((None, A_HEADS, tm, A_V), lambda b, i: (b, 0, i, 0)),
            _const_spec(w_oa.shape), _const_spec(g.shape), _const_spec(w_in.shape),
            _const_spec(ln_g.shape), _const_spec(ln_b.shape), _const_spec(w_s.shape),
            _const_spec(b_s.shape), _const_spec(w_out.shape), _const_spec(fg.shape),
        ],
        out_specs=pl.BlockSpec((None, tm, D_MODEL), lambda b, i: (b, i, 0)),
        out_shape=jax.ShapeDtypeStruct((bsz, seq, D_MODEL), jnp.float32),
        scratch_shapes=[pltpu.VMEM((tm, B_WIDTH), jnp.bfloat16)],
        compiler_params=pltpu.CompilerParams(
            dimension_semantics=("parallel", "parallel"), vmem_limit_bytes=VMEM_LIMIT),
        name="gmlp",
    )(x, og, w_oa, g, w_in, ln_g, ln_b, w_s, b_s, w_out, fg)


def _prep_mla(w_in, q_norm, kv_norm, w_q_up, w_kv_up, w_out):
    bf = jnp.bfloat16
    half = A_ROPE // 2
    c_kr = A_Q_LORA + A_KV_LORA
    w_kr = w_in[:, c_kr:c_kr + A_ROPE]
    w_kr_rot = jnp.concatenate([-w_kr[:, half:], w_kr[:, :half]], axis=1)
    zpad = jnp.zeros((D_MODEL, LANES - A_ROPE), w_in.dtype)
    w_in_p = jnp.concatenate(
        [w_in[:, :c_kr], w_in[:, c_kr + A_ROPE:], w_kr, zpad, w_kr_rot, zpad], axis=1).astype(bf)
    wq = w_q_up.reshape(A_Q_LORA, A_HEADS, A_NOPE + A_ROPE)
    wq_t = jnp.concatenate(
        [wq[:, :, :A_NOPE].reshape(A_Q_LORA, -1), wq[:, :, A_NOPE:].reshape(A_Q_LORA, -1)], axis=1).T.astype(bf)
    wkv = w_kv_up.reshape(A_KV_LORA, A_HEADS, A_NOPE + A_V)
    wk = wkv[:, :, :A_NOPE].reshape(A_KV_LORA, -1).astype(bf)
    wv_t = wkv[:, :, A_NOPE:].reshape(A_KV_LORA, -1).T.astype(bf)
    return (w_in_p, q_norm.reshape(1, -1), kv_norm.reshape(1, -1), wq_t, wk, wv_t, w_out.astype(bf))


def _prep_gmlp(w_in, ln_g, ln_b, w_s, b_s, w_out):
    bf = jnp.bfloat16
    b_rep = jnp.broadcast_to(b_s[:, :, None], (B_GROUPS, B_CHUNK, LANES))
    return (w_in.astype(bf), ln_g.reshape(1, -1), ln_b.reshape(1, -1), w_s.astype(bf), b_rep, w_out.astype(bf))


def _rope_tables(seq):
    inv = ROPE_THETA ** (-jnp.arange(0, A_ROPE, 2, dtype=jnp.float32) / A_ROPE)
    ang = jnp.arange(seq, dtype=jnp.float32)[:, None] * inv[None, :]
    ang = jnp.concatenate([ang, ang], axis=-1)
    cos, sin = jnp.cos(ang), jnp.sin(ang)
    half = A_ROPE // 2
    sign = jnp.concatenate([-jnp.ones((half,), jnp.float32), jnp.ones((half,), jnp.float32)])
    zpad = jnp.zeros((seq, LANES - A_ROPE), jnp.float32)
    cos_k = jnp.concatenate([cos, zpad], axis=1)
    sin_k = jnp.concatenate([sin, zpad], axis=1)
    return cos.T, (sin * sign).T, cos_k, sin_k


def _trunk(x, norm_g, final_g, mla_params, gmlp_params):
    seq = x.shape[1]
    tables = _rope_tables(seq)
    fg = final_g.reshape(1, -1)
    for j in range(DEPTH // 2):
        w_in_p, qn, kvn, wq_t, wk, wv_t, w_oa = mla_params[j]
        q_t, k, v_t, sg = _mla_proj(x, norm_g[2 * j].reshape(1, -1), w_in_p, qn, kvn, wq_t, wk, wv_t, *tables)
        og = _mla_attn(q_t, k, v_t, sg)
        x = _gmlp(x, og, w_oa, norm_g[2 * j + 1].reshape(1, -1), *gmlp_params[j], fg,
                  final_norm=(j == DEPTH // 2 - 1))
    return x


def kernel(x_prompt, x_sample, norm_g, final_g, a_w_in, a_q_norm, a_kv_norm, a_w_q_up, a_w_kv_up, a_w_out,
           b_w_in, b_ln_g, b_ln_b, b_w_s, b_b_s, b_w_out):
    mla_params = [_prep_mla(a_w_in[j], a_q_norm[j], a_kv_norm[j], a_w_q_up[j], a_w_kv_up[j], a_w_out[j])
                  for j in range(DEPTH // 2)]
    gmlp_params = [_prep_gmlp(b_w_in[j], b_ln_g[j], b_ln_b[j], b_w_s[j], b_b_s[j], b_w_out[j])
                   for j in range(DEPTH // 2)]
    y_prompt = _trunk(x_prompt, norm_g, final_g, mla_params, gmlp_params)
    y_sample = _trunk(x_sample, norm_g, final_g, mla_params, gmlp_params)
    return (y_prompt, y_sample)
```

```python
import functools
import math

import jax
import jax.numpy as jnp
from jax import lax
from jax.experimental import pallas as pl
from jax.experimental.pallas import tpu as pltpu

D_MODEL = 1024
DEPTH = 4

A_HEADS = 8
A_Q_LORA = 384
A_KV_LORA = 256
A_NOPE = 128
A_ROPE = 64
A_V = 128
A_WIDTH = A_HEADS * A_V
ROPE_THETA = 10000.0
A_QK_PAD = 256
A_V_AUG = A_V + 16

B_WIDTH = 2 * D_MODEL
B_GROUPS = 8
B_GROUP_DIM = B_WIDTH // B_GROUPS
B_CHUNK = 128

RMS_EPS = 1e-6
LN_EPS = 1e-5

LANES = 128
SUBLANES = 8
ATTN_TQ = 512
ATTN_TK = 1024
ATTN_DEPTH = 4
ATTN_B_LAG = 3
ATTN_C_LAG = 5
ATTN_RB = 64
PROJ_TM = ATTN_TQ
GMLP_TM = 512
VMEM_LIMIT = 56 * 1024 * 1024
ATTN_VMEM_BUDGET = 52 * 1024 * 1024

_C_Q = 0
_C_KV = _C_Q + A_Q_LORA
_C_GATE = _C_KV + A_KV_LORA
_C_KR = _C_GATE + A_WIDTH
_C_KRR = _C_KR + LANES
A_IN_PAD = _C_KRR + LANES

_NT = (((1,), (1,)), ((), ()))

EXP2_SCALE = (A_NOPE + A_ROPE) ** -0.5 * math.log2(math.e)


def _const_spec(shape):
    return pl.BlockSpec(shape, lambda *_: (0,) * len(shape), pipeline_mode=pl.Buffered(1))


def _rms(x, g):
    return (x * lax.rsqrt(jnp.mean(x * x, axis=-1, keepdims=True) + RMS_EPS)) * g


def _mla_proj_kernel(x_ref, g_ref, w_in_ref, qn_ref, kvn_ref, wq_t_ref, wk_ref, wv_t_ref,
                     cos_t_ref, sin_t_ref, cos_k_ref, sin_k_ref,
                     q_t_ref, k_ref, v_t_ref, sg_ref):
    hb = _rms(x_ref[...], g_ref[...]).astype(jnp.bfloat16)
    proj = jnp.dot(hb, w_in_ref[...], preferred_element_type=jnp.float32)
    gate = proj[:, _C_GATE:_C_KR]
    sg = gate * jax.nn.sigmoid(gate)

    q_lat = _rms(proj[:, _C_Q:_C_KV], qn_ref[...]).astype(jnp.bfloat16)
    kv_lat = _rms(proj[:, _C_KV:_C_GATE], kvn_ref[...]).astype(jnp.bfloat16)

    kr = (proj[:, _C_KR:_C_KRR] * cos_k_ref[...]
          + proj[:, _C_KRR:A_IN_PAD] * sin_k_ref[...]).astype(jnp.bfloat16)

    q_t = lax.dot_general(wq_t_ref[...], q_lat, _NT, preferred_element_type=jnp.float32)
    k_nope = jnp.dot(kv_lat, wk_ref[...], preferred_element_type=jnp.float32)
    v_t = lax.dot_general(wv_t_ref[...], kv_lat, _NT, preferred_element_type=jnp.float32)

    cos_t = cos_t_ref[...]
    sin_t = sin_t_ref[...]
    half = A_ROPE // 2
    rope0 = A_HEADS * A_NOPE
    for h in range(A_HEADS):
        k_ref[h, :, 0:A_NOPE] = k_nope[:, h * A_NOPE:(h + 1) * A_NOPE].astype(jnp.bfloat16)
        k_ref[h, :, A_NOPE:A_QK_PAD] = kr
        q_t_ref[h, 0:A_NOPE, :] = (q_t[h * A_NOPE:(h + 1) * A_NOPE, :] * EXP2_SCALE).astype(jnp.bfloat16)
        qr = q_t[rope0 + h * A_ROPE: rope0 + (h + 1) * A_ROPE, :]
        rot = jnp.concatenate([qr[half:, :], qr[:half, :]], axis=0)
        q_t_ref[h, A_NOPE:A_NOPE + A_ROPE, :] = ((qr * cos_t + rot * sin_t) * EXP2_SCALE).astype(jnp.bfloat16)
        q_t_ref[h, A_NOPE + A_ROPE:A_QK_PAD, :] = jnp.zeros(
            (A_QK_PAD - A_NOPE - A_ROPE, q_t.shape[1]), jnp.bfloat16)
        v_t_ref[h, 0:A_V, :] = v_t[h * A_V:(h + 1) * A_V, :].astype(jnp.bfloat16)
        sg_ref[h] = sg[:, h * A_V:(h + 1) * A_V]
        v_t_ref[h, A_V:A_V_AUG, :] = jnp.ones((A_V_AUG - A_V, v_t.shape[1]), jnp.bfloat16)


def _mla_proj(x, g, w_in, qn, kvn, wq_t, wk, wv_t, cos_t, sin_t, cos_k, sin_k):
    bsz, seq, _ = x.shape
    tm = PROJ_TM
    n_tiles = seq // tm
    tiles_per_chunk = ATTN_TK // tm
    return pl.pallas_call(
        _mla_proj_kernel,
        grid=(bsz, n_tiles),
        in_specs=[
            pl.BlockSpec((None, tm, D_MODEL), lambda b, i: (b, i, 0)),
            _const_spec(g.shape), _const_spec(w_in.shape), _const_spec(qn.shape), _const_spec(kvn.shape),
            _const_spec(wq_t.shape), _const_spec(wk.shape), _const_spec(wv_t.shape),
            pl.BlockSpec((A_ROPE, tm), lambda b, i: (0, i)),
            pl.BlockSpec((A_ROPE, tm), lambda b, i: (0, i)),
            pl.BlockSpec((tm, LANES), lambda b, i: (i, 0)),
            pl.BlockSpec((tm, LANES), lambda b, i: (i, 0)),
        ],
        out_specs=[
            pl.BlockSpec((None, A_HEADS, None, A_QK_PAD, tm), lambda b, i: (b, 0, i, 0, 0)),
            pl.BlockSpec((None, A_HEADS, tm, A_QK_PAD), lambda b, i: (b, 0, i, 0)),
            pl.BlockSpec((None, A_HEADS, None, A_V_AUG, tm), lambda b, i: (b, 0, i // tiles_per_chunk, 0, i % tiles_per_chunk)),
            pl.BlockSpec((None, A_HEADS, tm, A_V), lambda b, i: (b, 0, i, 0)),
        ],
        out_shape=[
            jax.ShapeDtypeStruct((bsz, A_HEADS, n_tiles, A_QK_PAD, tm), jnp.bfloat16),
            jax.ShapeDtypeStruct((bsz, A_HEADS, seq, A_QK_PAD), jnp.bfloat16),
            jax.ShapeDtypeStruct((bsz, A_HEADS, seq // ATTN_TK, A_V_AUG, ATTN_TK), jnp.bfloat16),
            jax.ShapeDtypeStruct((bsz, A_HEADS, seq, A_V), jnp.float32),
        ],
        compiler_params=pltpu.CompilerParams(
            dimension_semantics=("parallel", "parallel"), vmem_limit_bytes=VMEM_LIMIT),
        name="mla_proj",
    )(x, g, w_in, qn, kvn, wq_t, wk, wv_t, cos_t, sin_t, cos_k, sin_k)


def _mla_attn_kernel(q_t_ref, k_ref, v_t_ref, sg_ref, o_ref, *scratch):
    depth = ATTN_DEPTH
    s_bufs, p_bufs = scratch[0:depth], scratch[depth:2 * depth]
    cm_bufs, al_bufs = scratch[2 * depth:3 * depth], scratch[3 * depth:4 * depth]
    m_ref, acc_ref = scratch[4 * depth:]
    n_heads, n_tiles, _, tq = q_t_ref.shape
    _, n_chunks, v_rows, tk = v_t_ref.shape
    total = n_heads * n_tiles * n_chunks
    finish_in_loop = n_chunks == depth
    sub = m_ref.shape[0]

    def split(g):
        gt = g // n_chunks
        return gt // n_tiles, gt % n_tiles, g % n_chunks, gt

    m_ref[...] = jnp.zeros(m_ref.shape, jnp.float32)
    acc_ref[...] = jnp.zeros(acc_ref.shape, jnp.float32)

    def stage_a(g, slot):
        head, tile, chunk, _ = split(g)
        k_c = k_ref[head, pl.ds(pl.multiple_of(chunk * tk, tk), tk), :]
        s = jnp.dot(k_c, q_t_ref[head, tile], preferred_element_type=jnp.float32)
        s_bufs[slot][...] = s
        cm_bufs[slot][...] = jnp.max(s.reshape(tk // sub, sub, tq), axis=0)

    def stage_b(g, slot):
        _, _, chunk, _ = split(g)
        cm = cm_bufs[slot][...]
        for shift in (4, 2, 1):
            cm = jnp.maximum(cm, pltpu.roll(cm, shift, axis=0))
        m_old = jnp.where(chunk == 0, -jnp.inf, m_ref[...])
        m_new = jnp.maximum(m_old, cm)
        alpha = jnp.exp2(m_old - m_new)
        for r in range(tk // ATTN_RB):
            rows = slice(r * ATTN_RB, (r + 1) * ATTN_RB)
            s_blk = s_bufs[slot][rows, :].reshape(ATTN_RB // sub, sub, tq)
            p_blk = jnp.exp2(s_blk - m_new[None])
            p_bufs[slot][rows, :] = p_blk.reshape(ATTN_RB, tq).astype(jnp.bfloat16)
        al_bufs[slot][...] = alpha
        m_ref[...] = m_new

    def stage_c(g, slot):
        head, _, chunk, gt = split(g)
        pv = jnp.dot(v_t_ref[head, chunk], p_bufs[slot][...], preferred_element_type=jnp.float32)
        acc = acc_ref[gt].reshape(v_rows // sub, sub, tq) * al_bufs[slot][...][None]
        acc_ref[gt] = acc.reshape(v_rows, tq) + pv

    def finish(gt):
        head, tile = gt // n_tiles, gt % n_tiles
        rows = pl.ds(pl.multiple_of(tile * tq, tq), tq)
        acc = acc_ref[gt]
        l = acc[A_V:A_V + 1, :]
        o = (acc[0:A_V, :] / l).T
        o_ref[head, rows, :] = (o * sg_ref[head, rows, :]).astype(o_ref.dtype)

    def steps(tau0, static):
        for u in range(depth):
            g_b, g_c, g_a = tau0 + u - ATTN_B_LAG, tau0 + u - ATTN_C_LAG, tau0 + u
            if not static or 0 <= g_b < total:
                stage_b(g_b, (u - ATTN_B_LAG) % depth)
            if not static or 0 <= g_c < total:
                stage_c(g_c, (u - ATTN_C_LAG) % depth)
            if not static or 0 <= g_a < total:
                stage_a(g_a, u)
            if finish_in_loop and u == fin_u:
                gt = tau0 // depth - fin_shift
                if not static or 0 <= gt < n_heads * n_tiles:
                    finish(gt)

    fin_shift, fin_u = divmod(ATTN_C_LAG + depth - 1, depth)
    n_static = fin_shift
    for b in range(n_static):
        steps(b * depth, True)

    def body(i, carry):
        steps(i * depth, False)
        return carry

    lax.fori_loop(n_static, total // depth, body, 0)
    for tau0 in range(total, total + ATTN_C_LAG, depth):
        steps(tau0, True)

    if not finish_in_loop:
        def finish_body(gt, carry):
            finish(gt)
            return carry

        lax.fori_loop(0, n_heads * n_tiles, finish_body, 0)


def _heads_per_step(seq):
    per_head = 2 * seq * (2 * A_QK_PAD * 2 + A_V_AUG * 2 + A_V * 4 + A_V * 2) + seq * A_V_AUG * 4
    fixed = ATTN_DEPTH * ATTN_TK * ATTN_TQ * (4 + 2)
    hps = 1
    while hps < A_HEADS and 2 * hps * per_head + fixed <= ATTN_VMEM_BUDGET:
        hps *= 2
    return hps


def _mla_attn(q_t, k, v_t, sg):
    bsz, _, n_tiles, _, tq = q_t.shape
    seq = n_tiles * tq
    n_chunks = seq // ATTN_TK
    hps = _heads_per_step(seq)
    assert (hps * n_tiles * n_chunks) % ATTN_DEPTH == 0 and hps * n_tiles * n_chunks >= 4 * ATTN_DEPTH
    assert ATTN_B_LAG < ATTN_DEPTH and ATTN_B_LAG < ATTN_C_LAG <= ATTN_B_LAG + ATTN_DEPTH - 2
    f32, bf16 = jnp.float32, jnp.bfloat16
    return pl.pallas_call(
        _mla_attn_kernel,
        grid=(bsz, A_HEADS // hps),
        in_specs=[
            pl.BlockSpec((None, hps, n_tiles, A_QK_PAD, tq), lambda b, h: (b, h, 0, 0, 0)),
            pl.BlockSpec((None, hps, seq, A_QK_PAD), lambda b, h: (b, h, 0, 0)),
            pl.BlockSpec((None, hps, n_chunks, A_V_AUG, ATTN_TK), lambda b, h: (b, h, 0, 0, 0)),
            pl.BlockSpec((None, hps, seq, A_V), lambda b, h: (b, h, 0, 0)),
        ],
        out_specs=pl.BlockSpec((None, hps, seq, A_V), lambda b, h: (b, h, 0, 0)),
        out_shape=jax.ShapeDtypeStruct((bsz, A_HEADS, seq, A_V), bf16),
        scratch_shapes=(
            [pltpu.VMEM((ATTN_TK, tq), f32) for _ in range(ATTN_DEPTH)]
            + [pltpu.VMEM((ATTN_TK, tq), bf16) for _ in range(ATTN_DEPTH)]
            + [pltpu.VMEM((SUBLANES, tq), f32) for _ in range(ATTN_DEPTH)]
            + [pltpu.VMEM((SUBLANES, tq), f32) for _ in range(ATTN_DEPTH)]
            + [pltpu.VMEM((SUBLANES, tq), f32),
               pltpu.VMEM((hps * n_tiles, A_V_AUG, tq), f32)]
        ),
        compiler_params=pltpu.CompilerParams(
            dimension_semantics=("parallel", "parallel"), vmem_limit_bytes=VMEM_LIMIT),
        name="mla_attn",
    )(q_t, k, v_t, sg)


def _gelu(x):
    return 0.5 * x * (1.0 + lax.erf(x * (1.0 / math.sqrt(2.0))))


def _gmlp_kernel(x_ref, og_ref, w_oa_ref, g_ref, w_in_ref, ln_g_ref, ln_b_ref, w_s_ref, b_s_ref,
                 w_out_ref, fg_ref, y_ref, s_ref, *, final_norm):
    tm = x_ref.shape[0]
    og = jnp.concatenate([og_ref[h] for h in range(A_HEADS)], axis=1)
    x1 = x_ref[...] + jnp.dot(og, w_oa_ref[...], preferred_element_type=jnp.float32)
    hb = _rms(x1, g_ref[...]).astype(jnp.bfloat16)
    uvg = jnp.dot(hb, w_in_ref[...], preferred_element_type=jnp.float32)
    u = _gelu(uvg[:, :B_WIDTH])
    v = _gelu(uvg[:, B_WIDTH:2 * B_WIDTH])
    gate = uvg[:, 2 * B_WIDTH:]
    mu = jnp.mean(v, axis=-1, keepdims=True)
    vc = v - mu
    vn = (vc * lax.rsqrt(jnp.mean(vc * vc, axis=-1, keepdims=True) + LN_EPS)) * ln_g_ref[...] + ln_b_ref[...]
    vn = vn.astype(jnp.bfloat16)
    ug = u * (gate * jax.nn.sigmoid(gate))
    for c in range(tm // B_CHUNK):
        rows = slice(c * B_CHUNK, (c + 1) * B_CHUNK)
        for grp in range(B_GROUPS):
            cols = slice(grp * B_GROUP_DIM, (grp + 1) * B_GROUP_DIM)
            bias = b_s_ref[grp]
            sv = jnp.dot(w_s_ref[grp], vn[rows, cols], preferred_element_type=jnp.float32)
            sv = sv + jnp.concatenate([bias] * (B_GROUP_DIM // LANES), axis=1)
            s_ref[rows, cols] = (ug[rows, cols] * sv).astype(jnp.bfloat16)
    y = x1 + jnp.dot(s_ref[...], w_out_ref[...], preferred_element_type=jnp.float32)
    if final_norm:
        y = _rms(y, fg_ref[...])
    y_ref[...] = y


def _gmlp(x, og, w_oa, g, w_in, ln_g, ln_b, w_s, b_s, w_out, fg, *, final_norm):
    bsz, seq, _ = x.shape
    tm = GMLP_TM
    return pl.pallas_call(
        functools.partial(_gmlp_kernel, final_norm=final_norm),
        grid=(bsz, seq // tm),
        in_specs=[
            pl.BlockSpec((None, tm, D_MODEL), lambda b, i: (b, i, 0)),
            pl.BlockSpec((None, A_HEADS, tm, A_V), lambda b, i: (b, 0, i, 0)),
            _const_spec(w_oa.shape), _const_spec(g.shape), _const_spec(w_in.shape),
            _const_spec(ln_g.shape), _const_spec(ln_b.shape), _const_spec(w_s.shape),
            _const_spec(b_s.shape), _const_spec(w_out.shape), _const_spec(fg.shape),
        ],
        out_specs=pl.BlockSpec((None, tm, D_MODEL), lambda b, i: (b, i, 0)),
        out_shape=jax.ShapeDtypeStruct((bsz, seq, D_MODEL), jnp.float32),
        scratch_shapes=[pltpu.VMEM((tm, B_WIDTH), jnp.bfloat16)],
        compiler_params=pltpu.CompilerParams(
            dimension_semantics=("parallel", "parallel"), vmem_limit_bytes=VMEM_LIMIT),
        name="gmlp",
    )(x, og, w_oa, g, w_in, ln_g, ln_b, w_s, b_s, w_out, fg)


def _prep_mla(w_in, q_norm, kv_norm, w_q_up, w_kv_up, w_out):
    bf = jnp.bfloat16
    half = A_ROPE // 2
    c_kr = A_Q_LORA + A_KV_LORA
    w_kr = w_in[:, c_kr:c_kr + A_ROPE]
    w_kr_rot = jnp.concatenate([-w_kr[:, half:], w_kr[:, :half]], axis=1)
    zpad = jnp.zeros((D_MODEL, LANES - A_ROPE), w_in.dtype)
    w_in_p = jnp.concatenate(
        [w_in[:, :c_kr], w_in[:, c_kr + A_ROPE:], w_kr, zpad, w_kr_rot, zpad], axis=1).astype(bf)
    wq = w_q_up.reshape(A_Q_LORA, A_HEADS, A_NOPE + A_ROPE)
    wq_t = jnp.concatenate(
        [wq[:, :, :A_NOPE].reshape(A_Q_LORA, -1), wq[:, :, A_NOPE:].reshape(A_Q_LORA, -1)], axis=1).T.astype(bf)
    wkv = w_kv_up.reshape(A_KV_LORA, A_HEADS, A_NOPE + A_V)
    wk = wkv[:, :, :A_NOPE].reshape(A_KV_LORA, -1).astype(bf)
    wv_t = wkv[:, :, A_NOPE:].reshape(A_KV_LORA, -1).T.astype(bf)
    return (w_in_p, q_norm.reshape(1, -1), kv_norm.reshape(1, -1), wq_t, wk, wv_t, w_out.astype(bf))


def _prep_gmlp(w_in, ln_g, ln_b, w_s, b_s, w_out):
    bf = jnp.bfloat16
    b_rep = jnp.broadcast_to(b_s[:, :, None], (B_GROUPS, B_CHUNK, LANES))
    return (w_in.astype(bf), ln_g.reshape(1, -1), ln_b.reshape(1, -1), w_s.astype(bf), b_rep, w_out.astype(bf))


def _rope_tables(seq):
    inv = ROPE_THETA ** (-jnp.arange(0, A_ROPE, 2, dtype=jnp.float32) / A_ROPE)
    ang = jnp.arange(seq, dtype=jnp.float32)[:, None] * inv[None, :]
    ang = jnp.concatenate([ang, ang], axis=-1)
    cos, sin = jnp.cos(ang), jnp.sin(ang)
    half = A_ROPE // 2
    sign = jnp.concatenate([-jnp.ones((half,), jnp.float32), jnp.ones((half,), jnp.float32)])
    zpad = jnp.zeros((seq, LANES - A_ROPE), jnp.float32)
    cos_k = jnp.concatenate([cos, zpad], axis=1)
    sin_k = jnp.concatenate([sin, zpad], axis=1)
    return cos.T, (sin * sign).T, cos_k, sin_k


def _trunk(x, norm_g, final_g, mla_params, gmlp_params):
    seq = x.shape[1]
    tables = _rope_tables(seq)
    fg = final_g.reshape(1, -1)
    for j in range(DEPTH // 2):
        w_in_p, qn, kvn, wq_t, wk, wv_t, w_oa = mla_params[j]
        q_t, k, v_t, sg = _mla_proj(x, norm_g[2 * j].reshape(1, -1), w_in_p, qn, kvn, wq_t, wk, wv_t, *tables)
        og = _mla_attn(q_t, k, v_t, sg)
        x = _gmlp(x, og, w_oa, norm_g[2 * j + 1].reshape(1, -1), *gmlp_params[j], fg,
                  final_norm=(j == DEPTH // 2 - 1))
    return x


def kernel(x_prompt, x_sample, norm_g, final_g, a_w_in, a_q_norm, a_kv_norm, a_w_q_up, a_w_kv_up, a_w_out,
           b_w_in, b_ln_g, b_ln_b, b_w_s, b_b_s, b_w_out):
    mla_params = [_prep_mla(a_w_in[j], a_q_norm[j], a_kv_norm[j], a_w_q_up[j], a_w_kv_up[j], a_w_out[j])
                  for j in range(DEPTH // 2)]
    gmlp_params = [_prep_gmlp(b_w_in[j], b_ln_g[j], b_ln_b[j], b_w_s[j], b_b_s[j], b_w_out[j])
                   for j in range(DEPTH // 2)]
    y_prompt = _trunk(x_prompt, norm_g, final_g, mla_params, gmlp_params)
    y_sample = _trunk(x_sample, norm_g, final_g, mla_params, gmlp_params)
    return (y_prompt, y_sample)
```

```python
import functools
import math

import jax
import jax.numpy as jnp
from jax import lax
from jax.experimental import pallas as pl
from jax.experimental.pallas import tpu as pltpu

D_MODEL = 1024
DEPTH = 4

A_HEADS = 8
A_Q_LORA = 384
A_KV_LORA = 256
A_NOPE = 128
A_ROPE = 64
A_V = 128
A_WIDTH = A_HEADS * A_V
ROPE_THETA = 10000.0
A_QK_PAD = 256
A_V_AUG = A_V + 16

B_WIDTH = 2 * D_MODEL
B_GROUPS = 8
B_GROUP_DIM = B_WIDTH // B_GROUPS
B_CHUNK = 128

RMS_EPS = 1e-6
LN_EPS = 1e-5

LANES = 128
SUBLANES = 8
ATTN_TQ = 512
ATTN_TK = 1024
ATTN_DEPTH = 4
ATTN_B_LAG = 3
ATTN_C_LAG = 5
ATTN_RB = 64
PROJ_TM = ATTN_TQ
GMLP_TM = 512
VMEM_LIMIT = 56 * 1024 * 1024
ATTN_VMEM_BUDGET = 52 * 1024 * 1024

_C_Q = 0
_C_KV = _C_Q + A_Q_LORA
_C_GATE = _C_KV + A_KV_LORA
_C_KR = _C_GATE + A_WIDTH
_C_KRR = _C_KR + LANES
A_IN_PAD = _C_KRR + LANES

_NT = (((1,), (1,)), ((), ()))

EXP2_SCALE = (A_NOPE + A_ROPE) ** -0.5 * math.log2(math.e)


def _const_spec(shape):
    return pl.BlockSpec(shape, lambda *_: (0,) * len(shape), pipeline_mode=pl.Buffered(1))


def _rms(x, g):
    return (x * lax.rsqrt(jnp.mean(x * x, axis=-1, keepdims=True) + RMS_EPS)) * g


def _mla_proj_kernel(x_ref, g_ref, w_in_ref, qn_ref, kvn_ref, wq_t_ref, wk_ref, wv_t_ref,
                     cos_t_ref, sin_t_ref, cos_k_ref, sin_k_ref,
                     q_t_ref, k_ref, v_t_ref, sg_ref):
    hb = _rms(x_ref[...], g_ref[...]).astype(jnp.bfloat16)
    proj = jnp.dot(hb, w_in_ref[...], preferred_element_type=jnp.float32)
    gate = proj[:, _C_GATE:_C_KR]
    sg = gate * jax.nn.sigmoid(gate)

    q_lat = _rms(proj[:, _C_Q:_C_KV], qn_ref[...]).astype(jnp.bfloat16)
    kv_lat = _rms(proj[:, _C_KV:_C_GATE], kvn_ref[...]).astype(jnp.bfloat16)

    kr = (proj[:, _C_KR:_C_KRR] * cos_k_ref[...]
          + proj[:, _C_KRR:A_IN_PAD] * sin_k_ref[...]).astype(jnp.bfloat16)

    q_t = lax.dot_general(wq_t_ref[...], q_lat, _NT, preferred_element_type=jnp.float32)
    k_nope = jnp.dot(kv_lat, wk_ref[...], preferred_element_type=jnp.float32)
    v_t = lax.dot_general(wv_t_ref[...], kv_lat, _NT, preferred_element_type=jnp.float32)

    cos_t = cos_t_ref[...]
    sin_t = sin_t_ref[...]
    half = A_ROPE // 2
    rope0 = A_HEADS * A_NOPE
    for h in range(A_HEADS):
        k_ref[h, :, 0:A_NOPE] = k_nope[:, h * A_NOPE:(h + 1) * A_NOPE].astype(jnp.bfloat16)
        k_ref[h, :, A_NOPE:A_QK_PAD] = kr
        q_t_ref[h, 0:A_NOPE, :] = (q_t[h * A_NOPE:(h + 1) * A_NOPE, :] * EXP2_SCALE).astype(jnp.bfloat16)
        qr = q_t[rope0 + h * A_ROPE: rope0 + (h + 1) * A_ROPE, :]
        rot = jnp.concatenate([qr[half:, :], qr[:half, :]], axis=0)
        q_t_ref[h, A_NOPE:A_NOPE + A_ROPE, :] = ((qr * cos_t + rot * sin_t) * EXP2_SCALE).astype(jnp.bfloat16)
        q_t_ref[h, A_NOPE + A_ROPE:A_QK_PAD, :] = jnp.zeros(
            (A_QK_PAD - A_NOPE - A_ROPE, q_t.shape[1]), jnp.bfloat16)
        v_t_ref[h, 0:A_V, :] = v_t[h * A_V:(h + 1) * A_V, :].astype(jnp.bfloat16)
        sg_ref[h] = sg[:, h * A_V:(h + 1) * A_V]
        v_t_ref[h, A_V:A_V_AUG, :] = jnp.ones((A_V_AUG - A_V, v_t.shape[1]), jnp.bfloat16)


def _mla_proj(x, g, w_in, qn, kvn, wq_t, wk, wv_t, cos_t, sin_t, cos_k, sin_k):
    bsz, seq, _ = x.shape
    tm = PROJ_TM
    n_tiles = seq // tm
    tiles_per_chunk = ATTN_TK // tm
    return pl.pallas_call(
        _mla_proj_kernel,
        grid=(bsz, n_tiles),
        in_specs=[
            pl.BlockSpec((None, tm, D_MODEL), lambda b, i: (b, i, 0)),
            _const_spec(g.shape), _const_spec(w_in.shape), _const_spec(qn.shape), _const_spec(kvn.shape),
            _const_spec(wq_t.shape), _const_spec(wk.shape), _const_spec(wv_t.shape),
            pl.BlockSpec((A_ROPE, tm), lambda b, i: (0, i)),
            pl.BlockSpec((A_ROPE, tm), lambda b, i: (0, i)),
            pl.BlockSpec((tm, LANES), lambda b, i: (i, 0)),
            pl.BlockSpec((tm, LANES), lambda b, i: (i, 0)),
        ],
        out_specs=[
            pl.BlockSpec((None, A_HEADS, None, A_QK_PAD, tm), lambda b, i: (b, 0, i, 0, 0)),
            pl.BlockSpec((None, A_HEADS, tm, A_QK_PAD), lambda b, i: (b, 0, i, 0)),
            pl.BlockSpec((None, A_HEADS, None, A_V_AUG, tm), lambda b, i: (b, 0, i // tiles_per_chunk, 0, i % tiles_per_chunk)),
            pl.BlockSpec((None, A_HEADS, tm, A_V), lambda b, i: (b, 0, i, 0)),
        ],
        out_shape=[
            jax.ShapeDtypeStruct((bsz, A_HEADS, n_tiles, A_QK_PAD, tm), jnp.bfloat16),
            jax.ShapeDtypeStruct((bsz, A_HEADS, seq, A_QK_PAD), jnp.bfloat16),
            jax.ShapeDtypeStruct((bsz, A_HEADS, seq // ATTN_TK, A_V_AUG, ATTN_TK), jnp.bfloat16),
            jax.ShapeDtypeStruct((bsz, A_HEADS, seq, A_V), jnp.float32),
        ],
        compiler_params=pltpu.CompilerParams(
            dimension_semantics=("parallel", "parallel"), vmem_limit_bytes=VMEM_LIMIT),
        name="mla_proj",
    )(x, g, w_in, qn, kvn, wq_t, wk, wv_t, cos_t, sin_t, cos_k, sin_k)


def _mla_attn_kernel(q_t_ref, k_ref, v_t_ref, sg_ref, o_ref, *scratch):
    depth = ATTN_DEPTH
    s_bufs, p_bufs = scratch[0:depth], scratch[depth:2 * depth]
    cm_bufs, al_bufs = scratch[2 * depth:3 * depth], scratch[3 * depth:4 * depth]
    m_ref, acc_ref = scratch[4 * depth:]
    n_heads, n_tiles, _, tq = q_t_ref.shape
    _, n_chunks, v_rows, tk = v_t_ref.shape
    total = n_heads * n_tiles * n_chunks
    finish_in_loop = n_chunks == depth
    sub = m_ref.shape[0]

    def split(g):
        gt = g // n_chunks
        return gt // n_tiles, gt % n_tiles, g % n_chunks, gt

    m_ref[...] = jnp.zeros(m_ref.shape, jnp.float32)
    acc_ref[...] = jnp.zeros(acc_ref.shape, jnp.float32)

    def stage_a(g, slot):
        head, tile, chunk, _ = split(g)
        k_c = k_ref[head, pl.ds(pl.multiple_of(chunk * tk, tk), tk), :]
        s = jnp.dot(k_c, q_t_ref[head, tile], preferred_element_type=jnp.float32)
        s_bufs[slot][...] = s
        cm_bufs[slot][...] = jnp.max(s.reshape(tk // sub, sub, tq), axis=0)

    def stage_b(g, slot):
        _, _, chunk, _ = split(g)
        cm = cm_bufs[slot][...]
        for shift in (4, 2, 1):
            cm = jnp.maximum(cm, pltpu.roll(cm, shift, axis=0))
        m_old = jnp.where(chunk == 0, -jnp.inf, m_ref[...])
        m_new = jnp.maximum(m_old, cm)
        alpha = jnp.exp2(m_old - m_new)
        for r in range(tk // ATTN_RB):
            rows = slice(r * ATTN_RB, (r + 1) * ATTN_RB)
            s_blk = s_bufs[slot][rows, :].reshape(ATTN_RB // sub, sub, tq)
            p_blk = jnp.exp2(s_blk - m_new[None])
            p_bufs[slot][rows, :] = p_blk.reshape(ATTN_RB, tq).astype(jnp.bfloat16)
        al_bufs[slot][...] = alpha
        m_ref[...] = m_new

    def stage_c(g, slot):
        head, _, chunk, gt = split(g)
        pv = jnp.dot(v_t_ref[head, chunk], p_bufs[slot][...], preferred_element_type=jnp.float32)
        acc = acc_ref[gt].reshape(v_rows // sub, sub, tq) * al_bufs[slot][...][None]
        acc_ref[gt] = acc.reshape(v_rows, tq) + pv

    def finish(gt):
        head, tile = gt // n_tiles, gt % n_tiles
        rows = pl.ds(pl.multiple_of(tile * tq, tq), tq)
        acc = acc_ref[gt]
        l = acc[A_V:A_V + 1, :]
        o = (acc[0:A_V, :] / l).T
        o_ref[head, rows, :] = (o * sg_ref[head, rows, :]).astype(o_ref.dtype)

    def steps(tau0, static):
        for u in range(depth):
            g_b, g_c, g_a = tau0 + u - ATTN_B_LAG, tau0 + u - ATTN_C_LAG, tau0 + u
            if not static or 0 <= g_b < total:
                stage_b(g_b, (u - ATTN_B_LAG) % depth)
            if not static or 0 <= g_c < total:
                stage_c(g_c, (u - ATTN_C_LAG) % depth)
            if not static or 0 <= g_a < total:
                stage_a(g_a, u)
            if finish_in_loop and u == fin_u:
                gt = tau0 // depth - fin_shift
                if not static or 0 <= gt < n_heads * n_tiles:
                    finish(gt)

    fin_shift, fin_u = divmod(ATTN_C_LAG + depth - 1, depth)
    n_static = fin_shift
    for b in range(n_static):
        steps(b * depth, True)

    def body(i, carry):
        steps(i * depth, False)
        return carry

    lax.fori_loop(n_static, total // depth, body, 0)
    for tau0 in range(total, total + ATTN_C_LAG, depth):
        steps(tau0, True)

    if not finish_in_loop:
        def finish_body(gt, carry):
            finish(gt)
            return carry

        lax.fori_loop(0, n_heads * n_tiles, finish_body, 0)


def _heads_per_step(seq):
    per_head = 2 * seq * (2 * A_QK_PAD * 2 + A_V_AUG * 2 + A_V * 4 + A_V * 2) + seq * A_V_AUG * 4
    fixed = ATTN_DEPTH * ATTN_TK * ATTN_TQ * (4 + 2)
    hps = 1
    while hps < A_HEADS and 2 * hps * per_head + fixed <= ATTN_VMEM_BUDGET:
        hps *= 2
    return hps


def _mla_attn(q_t, k, v_t, sg):
    bsz, _, n_tiles, _, tq = q_t.shape
    seq = n_tiles * tq
    n_chunks = seq // ATTN_TK
    hps = _heads_per_step(seq)
    assert (hps * n_tiles * n_chunks) % ATTN_DEPTH == 0 and hps * n_tiles * n_chunks >= 4 * ATTN_DEPTH
    assert ATTN_B_LAG < ATTN_DEPTH and ATTN_B_LAG < ATTN_C_LAG <= ATTN_B_LAG + ATTN_DEPTH - 2
    f32, bf16 = jnp.float32, jnp.bfloat16
    return pl.pallas_call(
        _mla_attn_kernel,
        grid=(bsz, A_HEADS // hps),
        in_specs=[
            pl.BlockSpec((None, hps, n_tiles, A_QK_PAD, tq), lambda b, h: (b, h, 0, 0, 0)),
            pl.BlockSpec((None, hps, seq, A_QK_PAD), lambda b, h: (b, h, 0, 0)),
            pl.BlockSpec((None, hps, n_chunks, A_V_AUG, ATTN_TK), lambda b, h: (b, h, 0, 0, 0)),
            pl.BlockSpec((None, hps, seq, A_V), lambda b, h: (b, h, 0, 0)),
        ],
        out_specs=pl.BlockSpec((None, hps, seq, A_V), lambda b, h: (b, h, 0, 0)),
        out_shape=jax.ShapeDtypeStruct((bsz, A_HEADS, seq, A_V), bf16),
        scratch_shapes=(
            [pltpu.VMEM((ATTN_TK, tq), f32) for _ in range(ATTN_DEPTH)]
            + [pltpu.VMEM((ATTN_TK, tq), bf16) for _ in range(ATTN_DEPTH)]
            + [pltpu.VMEM((SUBLANES, tq), f32) for _ in range(ATTN_DEPTH)]
            + [pltpu.VMEM((SUBLANES, tq), f32) for _ in range(ATTN_DEPTH)]
            + [pltpu.VMEM((SUBLANES, tq), f32),
               pltpu.VMEM((hps * n_tiles, A_V_AUG, tq), f32)]
        ),
        compiler_params=pltpu.CompilerParams(
            dimension_semantics=("parallel", "parallel"), vmem_limit_bytes=VMEM_LIMIT),
        name="mla_attn",
    )(q_t, k, v_t, sg)


def _gelu(x):
    return 0.5 * x * (1.0 + lax.erf(x * (1.0 / math.sqrt(2.0))))


def _gmlp_kernel(x_ref, og_ref, w_oa_ref, g_ref, w_in_ref, ln_g_ref, ln_b_ref, w_s_ref, b_s_ref,
                 w_out_ref, fg_ref, y_ref, s_ref, *, final_norm):
    tm = x_ref.shape[0]
    og = jnp.concatenate([og_ref[h] for h in range(A_HEADS)], axis=1)
    x1 = x_ref[...] + jnp.dot(og, w_oa_ref[...], preferred_element_type=jnp.float32)
    hb = _rms(x1, g_ref[...]).astype(jnp.bfloat16)
    v = _gelu(jnp.dot(hb, w_in_ref[:, B_WIDTH:2 * B_WIDTH], preferred_element_type=jnp.float32))
    mu = jnp.mean(v, axis=-1, keepdims=True)
    vc = v - mu
    vn = (vc * lax.rsqrt(jnp.mean(vc * vc, axis=-1, keepdims=True) + LN_EPS)) * ln_g_ref[...] + ln_b_ref[...]
    vn = vn.astype(jnp.bfloat16)
    for grp in range(B_GROUPS):
        cols = slice(grp * B_GROUP_DIM, (grp + 1) * B_GROUP_DIM)
        gcols = slice(2 * B_WIDTH + grp * B_GROUP_DIM, 2 * B_WIDTH + (grp + 1) * B_GROUP_DIM)
        u = _gelu(jnp.dot(hb, w_in_ref[:, cols], preferred_element_type=jnp.float32))
        gate = jnp.dot(hb, w_in_ref[:, gcols], preferred_element_type=jnp.float32)
        ug = u * (gate * jax.nn.sigmoid(gate))
        bias = b_s_ref[grp]
        bias = jnp.concatenate([bias] * (B_GROUP_DIM // LANES), axis=1)
        for c in range(tm // B_CHUNK):
            rows = slice(c * B_CHUNK, (c + 1) * B_CHUNK)
            sv = jnp.dot(w_s_ref[grp], vn[rows, cols], preferred_element_type=jnp.float32) + bias
            s_ref[rows, cols] = (ug[rows, :] * sv).astype(jnp.bfloat16)
    y = x1 + jnp.dot(s_ref[...], w_out_ref[...], preferred_element_type=jnp.float32)
    if final_norm:
        y = _rms(y, fg_ref[...])
    y_ref[...] = y


def _gmlp(x, og, w_oa, g, w_in, ln_g, ln_b, w_s, b_s, w_out, fg, *, final_norm):
    bsz, seq, _ = x.shape
    tm = GMLP_TM
    return pl.pallas_call(
        functools.partial(_gmlp_kernel, final_norm=final_norm),
        grid=(bsz, seq // tm),
        in_specs=[
            pl.BlockSpec((None, tm, D_MODEL), lambda b, i: (b, i, 0)),
            pl.BlockSpec((None, A_HEADS, tm, A_V), lambda b, i: (b, 0, i, 0)),
            _const_spec(w_oa.shape), _const_spec(g.shape), _const_spec(w_in.shape),
            _const_spec(ln_g.shape), _const_spec(ln_b.shape), _const_spec(w_s.shape),
            _const_spec(b_s.shape), _const_spec(w_out.shape), _const_spec(fg.shape),
        ],
        out_specs=pl.BlockSpec((None, tm, D_MODEL), lambda b, i: (b, i, 0)),
        out_shape=jax.ShapeDtypeStruct((bsz, seq, D_MODEL), jnp.float32),
        scratch_shapes=[pltpu.VMEM((tm, B_WIDTH), jnp.bfloat16)],
        compiler_params=pltpu.CompilerParams(
            dimension_semantics=("parallel", "parallel"), vmem_limit_bytes=VMEM_LIMIT),
        name="gmlp",
    )(x, og, w_oa, g, w_in, ln_g, ln_b, w_s, b_s, w_out, fg)


def _prep_mla(w_in, q_norm, kv_norm, w_q_up, w_kv_up, w_out):
    bf = jnp.bfloat16
    half = A_ROPE // 2
    c_kr = A_Q_LORA + A_KV_LORA
    w_kr = w_in[:, c_kr:c_kr + A_ROPE]
    w_kr_rot = jnp.concatenate([-w_kr[:, half:], w_kr[:, :half]], axis=1)
    zpad = jnp.zeros((D_MODEL, LANES - A_ROPE), w_in.dtype)
    w_in_p = jnp.concatenate(
        [w_in[:, :c_kr], w_in[:, c_kr + A_ROPE:], w_kr, zpad, w_kr_rot, zpad], axis=1).astype(bf)
    wq = w_q_up.reshape(A_Q_LORA, A_HEADS, A_NOPE + A_ROPE)
    wq_t = jnp.concatenate(
        [wq[:, :, :A_NOPE].reshape(A_Q_LORA, -1), wq[:, :, A_NOPE:].reshape(A_Q_LORA, -1)], axis=1).T.astype(bf)
    wkv = w_kv_up.reshape(A_KV_LORA, A_HEADS, A_NOPE + A_V)
    wk = wkv[:, :, :A_NOPE].reshape(A_KV_LORA, -1).astype(bf)
    wv_t = wkv[:, :, A_NOPE:].reshape(A_KV_LORA, -1).T.astype(bf)
    return (w_in_p, q_norm.reshape(1, -1), kv_norm.reshape(1, -1), wq_t, wk, wv_t, w_out.astype(bf))


def _prep_gmlp(w_in, ln_g, ln_b, w_s, b_s, w_out):
    bf = jnp.bfloat16
    b_rep = jnp.broadcast_to(b_s[:, :, None], (B_GROUPS, B_CHUNK, LANES))
    return (w_in.astype(bf), ln_g.reshape(1, -1), ln_b.reshape(1, -1), w_s.astype(bf), b_rep, w_out.astype(bf))


def _rope_tables(seq):
    inv = ROPE_THETA ** (-jnp.arange(0, A_ROPE, 2, dtype=jnp.float32) / A_ROPE)
    ang = jnp.arange(seq, dtype=jnp.float32)[:, None] * inv[None, :]
    ang = jnp.concatenate([ang, ang], axis=-1)
    cos, sin = jnp.cos(ang), jnp.sin(ang)
    half = A_ROPE // 2
    sign = jnp.concatenate([-jnp.ones((half,), jnp.float32), jnp.ones((half,), jnp.float32)])
    zpad = jnp.zeros((seq, LANES - A_ROPE), jnp.float32)
    cos_k = jnp.concatenate([cos, zpad], axis=1)
    sin_k = jnp.concatenate([sin, zpad], axis=1)
    return cos.T, (sin * sign).T, cos_k, sin_k


def _trunk(x, norm_g, final_g, mla_params, gmlp_params):
    seq = x.shape[1]
    tables = _rope_tables(seq)
    fg = final_g.reshape(1, -1)
    for j in range(DEPTH // 2):
        w_in_p, qn, kvn, wq_t, wk, wv_t, w_oa = mla_params[j]
        q_t, k, v_t, sg = _mla_proj(x, norm_g[2 * j].reshape(1, -1), w_in_p, qn, kvn, wq_t, wk, wv_t, *tables)
        og = _mla_attn(q_t, k, v_t, sg)
        x = _gmlp(x, og, w_oa, norm_g[2 * j + 1].reshape(1, -1), *gmlp_params[j], fg,
                  final_norm=(j == DEPTH // 2 - 1))
    return x


def kernel(x_prompt, x_sample, norm_g, final_g, a_w_in, a_q_norm, a_kv_norm, a_w_q_up, a_w_kv_up, a_w_out,
           b_w_in, b_ln_g, b_ln_b, b_w_s, b_b_s, b_w_out):
    mla_params = [_prep_mla(a_w_in[j], a_q_norm[j], a_kv_norm[j], a_w_q_up[j], a_w_kv_up[j], a_w_out[j])
                  for j in range(DEPTH // 2)]
    gmlp_params = [_prep_gmlp(b_w_in[j], b_ln_g[j], b_ln_b[j], b_w_s[j], b_b_s[j], b_w_out[j])
                   for j in range(DEPTH // 2)]
    y_prompt = _trunk(x_prompt, norm_g, final_g, mla_params, gmlp_params)
    y_sample = _trunk(x_sample, norm_g, final_g, mla_params, gmlp_params)
    return (y_prompt, y_sample)
```
